```python
import jax, jax.numpy as jnp
from jax import lax
import numpy as np

D_MODEL = 1024
BATCH = 4
SEQ = 4096
DEPTH = 2

HEAD_DIM = 64
MOBA_HEADS = 4
MOBA_BLOCK = 256
MOBA_TOPK = 3
MOBA_QCHUNK = 128
LRU_WIDTH = 256
LRU_BLOCKS = 4
LRU_BLOCK_DIM = LRU_WIDTH // LRU_BLOCKS
CONV_WIDTH = 4
LRU_C = 8.0
DIL_HEADS = 4
DIL_CONFIGS = ((128, 1), (512, 4), (2048, 16))
DIL_QBLOCK = 128
GLA_HEADS = 4
GLA_DK = 32
GLA_DV = 64
GLA_LOWRANK = 16
GLA_TAU = 16.0
GLA_CHUNK = 32

ROPE_THETA = 10000.0
LN_EPS = 1e-5
NEG = -1e30
MOD_SCALE = 0.1

A_W = MOBA_HEADS * HEAD_DIM
B_W = LRU_WIDTH
C_W = DIL_HEADS * HEAD_DIM
D_KW = GLA_HEADS * GLA_DK
D_W = GLA_HEADS * GLA_DV
D_MIX = A_W + B_W + C_W + D_W
COLS = (('a_q', A_W), ('a_k', A_W), ('a_v', A_W), ('a_g', A_W),
        ('b_x', B_W), ('b_g', B_W),
        ('c_q', C_W), ('c_k', C_W), ('c_v', C_W), ('c_g', C_W),
        ('d_q', D_KW), ('d_k', D_KW), ('d_v', D_W), ('d_g', D_W), ('d_r', GLA_LOWRANK))
D_IN = sum(w for _, w in COLS)
DEEPNORM_ALPHA = (2 * DEPTH) ** 0.25
DEEPNORM_BETA = (8 * DEPTH) ** -0.25

kernel_name = 'hybrid_moba_rglru_dilated_gla_deepnorm'

F32 = jnp.float32


def _layer_norm(x, g=None, b=None):
    xf = x.astype(F32)
    mu = jnp.mean(xf, -1, keepdims=True)
    var = jnp.mean(jnp.square(xf - mu), -1, keepdims=True)
    y = (xf - mu) * lax.rsqrt(var + LN_EPS)
    if g is not None:
        y = y * g + b
    return y.astype(x.dtype)


def _softmax_lse(s):
    m = jnp.max(s, -1, keepdims=True)
    e = jnp.exp(s - m)
    zsum = jnp.sum(e, -1, keepdims=True)
    return e / zsum, (m + jnp.log(zsum))[..., 0]


def _heads(z, n, dh):
    B, S, _ = z.shape
    return z.reshape(B, S, n, dh)


def _to_bhsd(t):
    return t.transpose(0, 2, 1, 3)


def _rope(x, pos):
    half = x.shape[-1] // 2
    inv = ROPE_THETA ** (-jnp.arange(half, dtype=F32) / half)
    ang = pos.astype(F32)[..., None] * inv
    cos = jnp.cos(ang)[:, :, None, :]
    sin = jnp.sin(ang)[:, :, None, :]
    xf = x.astype(F32)
    x1, x2 = xf[..., :half], xf[..., half:]
    return jnp.concatenate([x1 * cos - x2 * sin, x1 * sin + x2 * cos], -1).astype(x.dtype)


def _moba(q, k, v):
    B, H, S, hd = q.shape
    nb = -(-S // MOBA_BLOCK)
    sp = nb * MOBA_BLOCK
    padw = ((0, 0), (0, 0), (0, sp - S), (0, 0))
    qp = jnp.pad(q, padw).astype(F32) * hd ** -0.5
    kp = jnp.pad(k, padw).astype(F32)
    vp = jnp.pad(v, padw).astype(F32)
    qb = qp.reshape(B, H, nb, MOBA_BLOCK, hd)
    kb = kp.reshape(B, H, nb, MOBA_BLOCK, hd)
    vb = vp.reshape(B, H, nb, MOBA_BLOCK, hd)
    causal = jnp.tril(jnp.ones((MOBA_BLOCK, MOBA_BLOCK), bool))
    s_own = jnp.where(causal, jnp.einsum('bhnqd,bhnkd->bhnqk', qb, kb), NEG)
    p_own, lse_own = _softmax_lse(s_own)
    o_own = jnp.einsum('bhnqk,bhnkd->bhnqd', p_own, vb).reshape(B, H, sp, hd)
    lse_own = lse_own.reshape(B, H, sp)
    topk = min(MOBA_TOPK, nb - 1)
    if topk == 0:
        return o_own[:, :, :S].astype(q.dtype)
    qblk = jnp.arange(sp) // MOBA_BLOCK
    k_mean = jnp.mean(kb, axis=3)
    gate = jnp.einsum('bhsd,bhnd->bhsn', qp, k_mean)
    gate = jnp.where(jnp.arange(nb)[None, :] < qblk[:, None], gate, NEG)
    _, sel = lax.top_k(gate, topk)
    valid = sel < qblk[:, None]
    nc = sp // MOBA_QCHUNK

    def to_chunks(t):
        return jnp.moveaxis(t.reshape(B, H, nc, MOBA_QCHUNK, t.shape[-1]), 2, 0)

    bi = jnp.arange(B)[:, None, None, None]
    hi = jnp.arange(H)[None, :, None, None]

    def chunk(args):
        qc, selc, validc = args
        kg = kb[bi, hi, selc]
        vg = vb[bi, hi, selc]
        s = jnp.einsum('bhqd,bhqnkd->bhqnk', qc, kg)
        s = jnp.where(validc[..., None], s, NEG).reshape(B, H, MOBA_QCHUNK, topk * MOBA_BLOCK)
        p, lse = _softmax_lse(s)
        o = jnp.einsum('bhqnk,bhqnkd->bhqd', p.reshape(B, H, MOBA_QCHUNK, topk, MOBA_BLOCK), vg)
        return o, lse

    o_sel, lse_sel = lax.map(chunk, (to_chunks(qp), to_chunks(sel), to_chunks(valid)))
    o_sel = jnp.moveaxis(o_sel, 0, 2).reshape(B, H, sp, hd)
    lse_sel = jnp.moveaxis(lse_sel, 0, 2).reshape(B, H, sp)
    m = jnp.maximum(lse_own, lse_sel)
    w_own = jnp.exp(lse_own - m)
    w_sel = jnp.exp(lse_sel - m)
    o = (w_own[..., None] * o_own + w_sel[..., None] * o_sel) / (w_own + w_sel)[..., None]
    return o[:, :, :S].astype(q.dtype)


def _dilated_branch(q, k, v, window, dil):
    B, H, S, hd = q.shape
    n_steps = window // dil
    QB = DIL_QBLOCK
    L = -(-S // (dil * QB)) * QB
    sp = L * dil
    nb = L // QB

    def regroup(t):
        t = jnp.pad(t, ((0, 0), (0, 0), (0, sp - S), (0, 0)))
        return t.reshape(B, H, L, dil, hd).transpose(0, 1, 3, 2, 4).reshape(B, H, dil, nb, QB, hd)

    def prev_block(t):
        return jnp.pad(t[:, :, :, :-1], ((0, 0), (0, 0), (0, 0), (1, 0), (0, 0), (0, 0)))

    qb, kb, vb = regroup(q), regroup(k), regroup(v)
    kk = jnp.concatenate([prev_block(kb), kb], axis=4)
    vv = jnp.concatenate([prev_block(vb), vb], axis=4)
    qi = jnp.arange(QB)[:, None]
    kj = jnp.arange(2 * QB)[None, :]
    dist = qi + QB - kj
    band = (dist >= 0) & (dist <= n_steps)
    first = (jnp.arange(nb) == 0)[:, None, None] & (kj < QB)[None]
    valid = band[None] & ~first
    s = jnp.where(valid, jnp.einsum('bhrnqd,bhrnkd->bhrnqk', qb, kk), NEG)
    p, lse = _softmax_lse(s)
    o = jnp.einsum('bhrnqk,bhrnkd->bhrnqd', p, vv)
    o = o.reshape(B, H, dil, L, hd).transpose(0, 1, 3, 2, 4).reshape(B, H, sp, hd)[:, :, :S]
    lse = lse.reshape(B, H, dil, L).transpose(0, 1, 3, 2).reshape(B, H, sp)[:, :, :S]
    return o, lse


def _dilated(q, k, v):
    hd = q.shape[-1]
    qf = q.astype(F32) * hd ** -0.5
    kf = k.astype(F32)
    vf = v.astype(F32)
    outs, lses = [], []
    for window, dil in DIL_CONFIGS:
        o, l = _dilated_branch(qf, kf, vf, window, dil)
        outs.append(o)
        lses.append(l)
    wts = jax.nn.softmax(jnp.stack(lses, 0), axis=0)
    o = jnp.einsum('cbhs,cbhsd->bhsd', wts, jnp.stack(outs, 0))
    return o.astype(q.dtype)


def _causal_conv(x, w, b):
    C = x.shape[-1]
    y = lax.conv_general_dilated(x, w[:, None, :], window_strides=(1,),
                                 padding=[(CONV_WIDTH - 1, 0)],
                                 dimension_numbers=('NWC', 'WIO', 'NWC'),
                                 feature_group_count=C)
    return y + b


def _rg_lru(x, w_a, b_a, w_x, b_x, lam):
    B, S, W = x.shape
    xb = x.reshape(B, S, LRU_BLOCKS, LRU_BLOCK_DIM)
    r = jax.nn.sigmoid(jnp.einsum('bsgi,gio->bsgo', xb, w_a).reshape(B, S, W) + b_a).astype(F32)
    i = jax.nn.sigmoid(jnp.einsum('bsgi,gio->bsgo', xb, w_x).reshape(B, S, W) + b_x).astype(F32)
    log_a = -LRU_C * r * jax.nn.softplus(-lam.astype(F32))
    a = jnp.exp(log_a)
    u = jnp.sqrt(-jnp.expm1(2.0 * log_a)) * (i * x.astype(F32))

    def combine(left, right):
        a1, b1 = left
        a2, b2 = right
        return a1 * a2, a2 * b1 + b2

    _, h = lax.associative_scan(combine, (a, u), axis=1)
    return h.astype(x.dtype)


def _gla(q, k, v, log_alpha):
    B, H, S, dk = q.shape
    dv = v.shape[-1]
    C = GLA_CHUNK
    nc = S // C

    def rs(t):
        return t.astype(F32).reshape(B, H, nc, C, t.shape[-1])

    qc = rs(q) * dk ** -0.5
    kc, vc, gc = rs(k), rs(v), rs(log_alpha)
    bcum = jnp.cumsum(gc, axis=3)
    causal = jnp.tril(jnp.ones((C, C), bool))
    diff = bcum[..., :, None, :] - bcum[..., None, :, :]
    decay = jnp.exp(jnp.where(causal[..., None], diff, NEG))
    A = jnp.einsum('bhnid,bhnjd,bhnijd->bhnij', qc, kc, decay)
    o_intra = jnp.einsum('bhnij,bhnjv->bhniv', A, vc)
    q_in = qc * jnp.exp(bcum)
    b_last = bcum[..., -1:, :]
    kv = jnp.einsum('bhnjd,bhnjv->bhndv', kc * jnp.exp(b_last - bcum), vc)
    chunk_decay = jnp.exp(b_last[..., 0, :])

    def step(state, inp):
        qi, kvi, dec = inp
        o = jnp.einsum('bhid,bhdv->bhiv', qi, state)
        return dec[..., None] * state + kvi, o

    xs = (jnp.moveaxis(q_in, 2, 0), jnp.moveaxis(kv, 2, 0), jnp.moveaxis(chunk_decay, 2, 0))
    _, o_inter = lax.scan(step, jnp.zeros((B, H, dk, dv), F32), xs)
    o = o_intra + jnp.moveaxis(o_inter, 0, 2)
    return o.reshape(B, H, S, dv)


def _layer(x, c, pos, w_mod, b_mod, w_in, conv_w, conv_b, lru_wa, lru_ba, lru_wx, lru_bx,
           lru_lam, gla_wr, gla_br, gla_gn, w_out, ln_g, ln_b):
    B, S, _ = x.shape
    shift, scale, gate = jnp.split(c @ w_mod + b_mod, 3, axis=-1)
    u = _layer_norm(x) * (1 + scale[:, None]) + shift[:, None]
    z = u @ w_in
    p = {}
    off = 0
    for name, w in COLS:
        p[name] = z[..., off:off + w]
        off += w
    qa = _rope(_heads(p['a_q'], MOBA_HEADS, HEAD_DIM), pos)
    ka = _rope(_heads(p['a_k'], MOBA_HEADS, HEAD_DIM), pos)
    va = _heads(p['a_v'], MOBA_HEADS, HEAD_DIM)
    ya = _moba(_to_bhsd(qa), _to_bhsd(ka), _to_bhsd(va)).transpose(0, 2, 1, 3).reshape(B, S, A_W)
    yb = _rg_lru(_causal_conv(p['b_x'], conv_w, conv_b), lru_wa, lru_ba, lru_wx, lru_bx, lru_lam)
    qc = _rope(_heads(p['c_q'], DIL_HEADS, HEAD_DIM), pos)
    kc = _rope(_heads(p['c_k'], DIL_HEADS, HEAD_DIM), pos)
    vc = _heads(p['c_v'], DIL_HEADS, HEAD_DIM)
    yc = _dilated(_to_bhsd(qc), _to_bhsd(kc), _to_bhsd(vc)).transpose(0, 2, 1, 3).reshape(B, S, C_W)
    log_alpha = jax.nn.log_sigmoid((p['d_r'] @ gla_wr + gla_br).astype(F32)) / GLA_TAU
    yd = _gla(_to_bhsd(_heads(p['d_q'], GLA_HEADS, GLA_DK)),
              _to_bhsd(_heads(p['d_k'], GLA_HEADS, GLA_DK)),
              _to_bhsd(_heads(p['d_v'], GLA_HEADS, GLA_DV)),
              _to_bhsd(_heads(log_alpha, GLA_HEADS, GLA_DK)))
    yd = yd * lax.rsqrt(jnp.mean(yd * yd, -1, keepdims=True) + LN_EPS) * gla_gn.astype(F32)
    yd = yd.transpose(0, 2, 1, 3).reshape(B, S, D_W).astype(x.dtype)
    mix = jnp.concatenate([ya * jax.nn.silu(p['a_g']), yb * jax.nn.silu(p['b_g']),
                           yc * jax.nn.silu(p['c_g']), yd * jax.nn.silu(p['d_g'])], -1)
    y = mix @ w_out
    return _layer_norm(DEEPNORM_ALPHA * x + (1 + gate[:, None]) * y, ln_g, ln_b)


def setup_inputs(seed: int = 0) -> dict:
    key = jax.random.key(seed)
    ks = jax.random.split(key, 20)

    def nrm(k, shape, s):
        return jax.random.normal(k, shape, F32) * s

    x = nrm(ks[0], (BATCH, SEQ, D_MODEL), 1.0)
    c = nrm(ks[1], (BATCH, D_MODEL), 1.0)
    positions = (jnp.arange(SEQ, dtype=jnp.int32)[None, :]
                 + jax.random.randint(ks[2], (BATCH, 1), 0, 1024, dtype=jnp.int32))
    w_mod = nrm(ks[3], (DEPTH, D_MODEL, 3 * D_MODEL), MOD_SCALE * D_MODEL ** -0.5)
    b_mod = nrm(ks[4], (DEPTH, 3 * D_MODEL), 0.01)
    w_in = nrm(ks[5], (DEPTH, D_MODEL, D_IN), D_MODEL ** -0.5)
    conv_w = nrm(ks[6], (DEPTH, CONV_WIDTH, LRU_WIDTH), CONV_WIDTH ** -0.5)
    conv_b = nrm(ks[7], (DEPTH, LRU_WIDTH), 0.01)
    lru_wa = nrm(ks[8], (DEPTH, LRU_BLOCKS, LRU_BLOCK_DIM, LRU_BLOCK_DIM), LRU_BLOCK_DIM ** -0.5)
    lru_ba = nrm(ks[9], (DEPTH, LRU_WIDTH), 0.01)
    lru_wx = nrm(ks[10], (DEPTH, LRU_BLOCKS, LRU_BLOCK_DIM, LRU_BLOCK_DIM), LRU_BLOCK_DIM ** -0.5)
    lru_bx = nrm(ks[11], (DEPTH, LRU_WIDTH), 0.01)
    a_c = jax.random.uniform(ks[12], (DEPTH, LRU_WIDTH), F32, 0.9, 0.999)
    s = a_c ** (1.0 / LRU_C)
    lru_lam = jnp.log(s) - jnp.log1p(-s)
    gla_wr = nrm(ks[13], (DEPTH, GLA_LOWRANK, D_KW), GLA_LOWRANK ** -0.5)
    gla_br = nrm(ks[14], (DEPTH, D_KW), 0.1)
    gla_gn = 1.0 + nrm(ks[15], (DEPTH, GLA_DV), 0.02)
    w_out = nrm(ks[16], (DEPTH, D_MIX, D_MODEL), DEEPNORM_BETA * D_MIX ** -0.5)
    ln_g = 1.0 + nrm(ks[17], (DEPTH, D_MODEL), 0.02)
    ln_b = nrm(ks[18], (DEPTH, D_MODEL), 0.02)
    return {'x': x, 'c': c, 'positions': positions, 'w_mod': w_mod, 'b_mod': b_mod,
            'w_in': w_in, 'conv_w': conv_w, 'conv_b': conv_b, 'lru_wa': lru_wa,
            'lru_ba': lru_ba, 'lru_wx': lru_wx, 'lru_bx': lru_bx, 'lru_lam': lru_lam,
            'gla_wr': gla_wr, 'gla_br': gla_br, 'gla_gn': gla_gn, 'w_out': w_out,
            'ln_g': ln_g, 'ln_b': ln_b}


def reference(x, c, positions, w_mod, b_mod, w_in, conv_w, conv_b, lru_wa, lru_ba, lru_wx,
              lru_bx, lru_lam, gla_wr, gla_br, gla_gn, w_out, ln_g, ln_b):
    for l in range(DEPTH):
        x = _layer(x, c, positions, w_mod[l], b_mod[l], w_in[l], conv_w[l], conv_b[l],
                   lru_wa[l], lru_ba[l], lru_wx[l], lru_bx[l], lru_lam[l], gla_wr[l],
                   gla_br[l], gla_gn[l], w_out[l], ln_g[l], ln_b[l])
    return x
```

```python
import functools

import jax
import jax.numpy as jnp
from jax import lax
from jax.experimental import pallas as pl
from jax.experimental.pallas import tpu as pltpu

F32 = jnp.float32
BF16 = jnp.bfloat16
HIGHEST = lax.Precision.HIGHEST

HEAD_DIM = 64
MOBA_HEADS = 4
MOBA_BLOCK = 256
MOBA_TOPK = 3
LRU_WIDTH = 256
LRU_BLOCKS = 4
CONV_WIDTH = 4
LRU_C = 8.0
DIL_HEADS = 4
DIL_CONFIGS = ((128, 1), (512, 4), (2048, 16))
DIL_QBLOCK = 128
GLA_HEADS = 4
GLA_DK = 32
GLA_DV = 64
GLA_LOWRANK = 16
GLA_TAU = 16.0
GLA_CHUNK = 32
ROPE_THETA = 10000.0
LN_EPS = 1e-5
NEG = -1e30

A_W = MOBA_HEADS * HEAD_DIM
B_W = LRU_WIDTH
C_W = DIL_HEADS * HEAD_DIM
D_KW = GLA_HEADS * GLA_DK
D_W = GLA_HEADS * GLA_DV
OFF_AQ, OFF_AK, OFF_AV, OFF_AG = 0, A_W, 2 * A_W, 3 * A_W
OFF_BX = 4 * A_W
OFF_BG = OFF_BX + B_W
OFF_CQ = OFF_BG + B_W
OFF_CK, OFF_CV, OFF_CG = OFF_CQ + C_W, OFF_CQ + 2 * C_W, OFF_CQ + 3 * C_W
OFF_DQ = OFF_CQ + 4 * C_W
OFF_DK = OFF_DQ + D_KW
OFF_DV = OFF_DK + D_KW
OFF_DG = OFF_DV + D_W
OFF_DR = OFF_DG + D_W
D_IN = OFF_DR + GLA_LOWRANK

V7X_LANES = 128
V7X_SUBLANES = 8
V7X_VMEM_LIMIT_BYTES = 48 * 1024 * 1024

IN_TM = 512
OUT_TM = 512
LRU_TL = 512
GLA_TG = 256
ROPE_TR = 512
DIL_TQ = 256


def _params(*sem):
    return pltpu.CompilerParams(dimension_semantics=sem, vmem_limit_bytes=V7X_VMEM_LIMIT_BYTES)


def _sigmoid(x):
    return 1.0 / (1.0 + jnp.exp(-x))


def _log_sigmoid(x):
    return jnp.minimum(x, 0.0) - jnp.log1p(jnp.exp(-jnp.abs(x)))


def _dot_nt(a, b, **kw):
    return lax.dot_general(a, b, (((1,), (1,)), ((), ())), preferred_element_type=F32, **kw)


def _mod_kernel(c_ref, w_ref, b_ref, o_ref):
    o_ref[0] = jnp.dot(c_ref[...], w_ref[0], preferred_element_type=F32, precision=HIGHEST) + b_ref[0]


def _modulation(c, w_mod, b_mod):
    depth, d, d3 = w_mod.shape
    b = c.shape[0]
    nt = d3 // d
    return pl.pallas_call(
        _mod_kernel,
        out_shape=jax.ShapeDtypeStruct((depth, b, d3), F32),
        grid=(depth, nt),
        in_specs=[pl.BlockSpec((b, d), lambda l, j: (0, 0)),
                  pl.BlockSpec((1, d, d), lambda l, j: (l, 0, j)),
                  pl.BlockSpec((1, 1, d), lambda l, j: (l, 0, j))],
        out_specs=pl.BlockSpec((1, b, d), lambda l, j: (l, 0, j)),
        compiler_params=_params("parallel", "parallel"),
        name="adaln_modulation",
    )(c, w_mod, b_mod.reshape(depth, 1, d3))


def _rope_table_kernel(pos_ref, inv_ref, sgn_ref, cos_ref, sin_ref):
    ang = pos_ref[0].astype(F32) * inv_ref[...]
    cos_ref[0] = jnp.cos(ang)
    sin_ref[0] = jnp.sin(ang) * sgn_ref[...]


def _rope_tables(positions):
    b, s = positions.shape
    half = HEAD_DIM // 2
    inv = ROPE_THETA ** (-jnp.arange(half, dtype=F32) / half)
    reps = V7X_LANES // half
    inv_row = jnp.tile(inv, reps).reshape(1, V7X_LANES)
    sgn_row = jnp.tile(jnp.concatenate([-jnp.ones(half, F32), jnp.ones(half, F32)]), reps // 2).reshape(1, V7X_LANES)
    tab = jax.ShapeDtypeStruct((b, s, V7X_LANES), F32)
    row = pl.BlockSpec((1, V7X_LANES), lambda i, j: (0, 0))
    out = pl.BlockSpec((1, ROPE_TR, V7X_LANES), lambda i, j: (i, j, 0))
    return pl.pallas_call(
        _rope_table_kernel,
        out_shape=(tab, tab),
        grid=(b, s // ROPE_TR),
        in_specs=[pl.BlockSpec((1, ROPE_TR, 1), lambda i, j: (i, j, 0)), row, row],
        out_specs=(out, out),
        compiler_params=_params("parallel", "parallel"),
        name="rope_tables",
    )(positions.reshape(b, s, 1), inv_row, sgn_row)


def _rope(z, cos, sin):
    lane = lax.broadcasted_iota(jnp.int32, (1, V7X_LANES), 1)
    first_half = (lane % HEAD_DIM) < (HEAD_DIM // 2)
    outs = []
    for s in range(z.shape[1] // V7X_LANES):
        zs = z[:, s * V7X_LANES:(s + 1) * V7X_LANES]
        swapped = jnp.where(first_half, pltpu.roll(zs, V7X_LANES - HEAD_DIM // 2, 1), pltpu.roll(zs, HEAD_DIM // 2, 1))
        outs.append(zs * cos + swapped * sin)
    return jnp.concatenate(outs, axis=-1)


def _inproj_kernel(x_ref, mod_ref, w_ref, wvt_ref, cos_ref, sin_ref, wr_ref, br_ref,
                   aqk_ref, avt_ref, cq_ref, ck_ref, cv_ref, zg_ref, zb_ref, zd_ref):
    d = x_ref.shape[2]
    x = x_ref[0]
    mu = jnp.mean(x, axis=-1, keepdims=True)
    xc = x - mu
    var = jnp.mean(xc * xc, axis=-1, keepdims=True)
    xn = xc * lax.rsqrt(var + LN_EPS)
    shift = mod_ref[0, :, 0:d]
    scale = mod_ref[0, :, d:2 * d]
    u = (xn * (1.0 + scale) + shift).astype(BF16)

    def proj(off, width):
        return jnp.dot(u, w_ref[:, off:off + width], preferred_element_type=F32)

    cos = cos_ref[0]
    sin = sin_ref[0]
    q_scale = HEAD_DIM ** -0.5
    aqk_ref[0, :, 0:A_W] = (_rope(proj(OFF_AQ, A_W), cos, sin) * q_scale).astype(BF16)
    aqk_ref[0, :, A_W:2 * A_W] = _rope(proj(OFF_AK, A_W), cos, sin).astype(BF16)
    avt_ref[0] = _dot_nt(wvt_ref[...], u).astype(BF16)
    cq_ref[0] = (_rope(proj(OFF_CQ, C_W), cos, sin) * q_scale).astype(BF16)
    ck_ref[0] = _rope(proj(OFF_CK, C_W), cos, sin).astype(BF16)
    cv_ref[0] = proj(OFF_CV, C_W).astype(BF16)
    for j, off in enumerate((OFF_AG, OFF_BG, OFF_CG, OFF_DG)):
        g = proj(off, A_W)
        zg_ref[0, :, j * A_W:(j + 1) * A_W] = g * _sigmoid(g)
    zb_ref[0] = proj(OFF_BX, B_W)
    zd_ref[0, :, 0:D_KW] = proj(OFF_DQ, D_KW) * (GLA_DK ** -0.5)
    zd_ref[0, :, D_KW:2 * D_KW] = proj(OFF_DK, D_KW)
    zd_ref[0, :, 2 * D_KW:2 * D_KW + D_W] = proj(OFF_DV, D_W)
    dr = proj(OFF_DR, GLA_LOWRANK)
    la = jnp.dot(dr, wr_ref[...], preferred_element_type=F32, precision=HIGHEST) + br_ref[...]
    zd_ref[0, :, 2 * D_KW + D_W:] = _log_sigmoid(la) * (1.0 / GLA_TAU)


def _in_projection(x, mod_l, w_in_bf, wvt_bf, cos_t, sin_t, gla_wr, gla_br):
    b, s, d = x.shape
    tm = IN_TM
    row3 = lambda w: pl.BlockSpec((1, tm, w), lambda i, j: (i, j, 0))
    const2 = lambda shp: pl.BlockSpec(shp, lambda i, j: (0, 0))
    out_shape = (
        jax.ShapeDtypeStruct((b, s, 2 * A_W), BF16),
        jax.ShapeDtypeStruct((b, A_W, s), BF16),
        jax.ShapeDtypeStruct((b, s, C_W), BF16),
        jax.ShapeDtypeStruct((b, s, C_W), BF16),
        jax.ShapeDtypeStruct((b, s, C_W), BF16),
        jax.ShapeDtypeStruct((b, s, 4 * A_W), F32),
        jax.ShapeDtypeStruct((b, s, B_W), F32),
        jax.ShapeDtypeStruct((b, s, 2 * D_KW + D_W + D_KW), F32),
    )
    out_specs = (row3(2 * A_W), pl.BlockSpec((1, A_W, tm), lambda i, j: (i, 0, j)),
                 row3(C_W), row3(C_W), row3(C_W), row3(4 * A_W), row3(B_W), row3(2 * D_KW + D_W + D_KW))
    return pl.pallas_call(
        _inproj_kernel,
        out_shape=out_shape,
        grid=(b, s // tm),
        in_specs=[row3(d),
                  pl.BlockSpec((1, 1, mod_l.shape[-1]), lambda i, j: (i, 0, 0)),
                  const2(w_in_bf.shape), const2(wvt_bf.shape),
                  row3(V7X_LANES), row3(V7X_LANES),
                  const2(gla_wr.shape), const2((1, D_KW))],
        out_specs=out_specs,
        compiler_params=_params("parallel", "parallel"),
        name="in_projection",
    )(x, mod_l.reshape(b, 1, -1), w_in_bf, wvt_bf, cos_t, sin_t, gla_wr, gla_br.reshape(1, D_KW))


def _moba_kernel(q_ref, k_ref, vt_ref, o_ref, kmean_ref, acc_ref):
    blk = MOBA_BLOCK
    nb = k_ref.shape[1] // blk
    n = pl.program_id(1)

    @pl.when(n == 0)
    def _():
        for j in range(nb):
            kj = k_ref[0, j * blk:(j + 1) * blk, :].astype(F32)
            kmean_ref[j:j + 1, :] = jnp.mean(kj, axis=0, keepdims=True)

    q = q_ref[0]
    lane_head = lax.broadcasted_iota(jnp.int32, (1, A_W), 1) // HEAD_DIM
    km = kmean_ref[...]
    km_hi = km.astype(BF16)
    km_lo = (km - km_hi.astype(F32)).astype(BF16)
    blk_id = lax.broadcasted_iota(jnp.int32, (nb, blk), 0)
    key_idx = lax.broadcasted_iota(jnp.int32, (blk, blk), 0)
    qry_idx = lax.broadcasted_iota(jnp.int32, (blk, blk), 1)
    causal = key_idx <= qry_idx

    for h in range(MOBA_HEADS):
        qh = jnp.where(lane_head == h, q, jnp.zeros_like(q))
        gate = _dot_nt(km_hi, qh) + _dot_nt(km_lo, qh)
        gate = jnp.where(blk_id < n, gate, NEG)
        rank = jnp.zeros((nb, blk), jnp.int32)
        for i in range(nb):
            gi = gate[i:i + 1, :]
            tie = (blk_id > i).astype(jnp.int32)
            rank = rank + jnp.where(gi > gate, 1, jnp.where(gi == gate, tie, 0))
        sel_bias = jnp.where((rank < MOBA_TOPK) & (blk_id < n), 0.0, NEG)

        vth = lambda j: vt_ref[0, h * HEAD_DIM:(h + 1) * HEAD_DIM, pl.ds(pl.multiple_of(j * blk, blk), blk)]
        kblk = lambda j: k_ref[0, pl.ds(pl.multiple_of(j * blk, blk), blk), :]

        s = jnp.where(causal, _dot_nt(kblk(n), qh), NEG)
        m = jnp.max(s, axis=0, keepdims=True)
        p = jnp.exp(s - m)
        l = jnp.sum(p, axis=0, keepdims=True)
        acc = jnp.dot(vth(n), p.astype(BF16), preferred_element_type=F32)

        def body(j, carry):
            m, l, acc = carry
            bias = jnp.sum(jnp.where(blk_id == j, sel_bias, 0.0), axis=0, keepdims=True)
            s = _dot_nt(kblk(j), qh) + bias
            m_new = jnp.maximum(m, jnp.max(s, axis=0, keepdims=True))
            alpha = jnp.exp(m - m_new)
            p = jnp.exp(s - m_new)
            l = alpha * l + jnp.sum(p, axis=0, keepdims=True)
            acc = alpha * acc + jnp.dot(vth(j), p.astype(BF16), preferred_element_type=F32)
            return m_new, l, acc

        m, l, acc = lax.fori_loop(0, n, body, (m, l, acc))
        acc_ref[h * HEAD_DIM:(h + 1) * HEAD_DIM, :] = acc / l
    o_ref[0] = acc_ref[...].T


def _moba(aqk, avt):
    b, s, _ = aqk.shape
    blk = MOBA_BLOCK
    return pl.pallas_call(
        _moba_kernel,
        out_shape=jax.ShapeDtypeStruct((b, s, A_W), F32),
        grid=(b, s // blk),
        in_specs=[pl.BlockSpec((1, blk, A_W), lambda i, n: (i, n, 0)),
                  pl.BlockSpec((1, s, A_W), lambda i, n: (i, 0, 1)),
                  pl.BlockSpec((1, A_W, s), lambda i, n: (i, 0, 0))],
        out_specs=pl.BlockSpec((1, blk, A_W), lambda i, n: (i, n, 0)),
        scratch_shapes=[pltpu.VMEM((s // blk, A_W), F32), pltpu.VMEM((A_W, blk), F32)],
        compiler_params=_params("arbitrary", "arbitrary"),
        name="moba_attention",
    )(aqk, aqk, avt)


def _lru_kernel(x_ref, cw_ref, cb_ref, wg_ref, bg_ref, lam_ref, o_ref, xs_ref, h_ref):
    tl = x_ref.shape[1]
    halo = V7X_SUBLANES
    i = pl.program_id(1)

    @pl.when(i == 0)
    def _():
        xs_ref[0:halo, :] = jnp.zeros((halo, B_W), F32)
        h_ref[...] = jnp.zeros_like(h_ref)

    x = x_ref[0]
    xs_ref[halo:, :] = x
    y = cb_ref[...] + cw_ref[CONV_WIDTH - 1:CONV_WIDTH, :] * x
    for k in range(CONV_WIDTH - 1):
        y = y + cw_ref[k:k + 1, :] * xs_ref[pl.ds(halo - (CONV_WIDTH - 1) + k, tl), :]
    xs_ref[0:halo, :] = x[tl - halo:, :]

    gates = jnp.dot(y, wg_ref[...], preferred_element_type=F32, precision=HIGHEST) + bg_ref[...]
    r = _sigmoid(gates[:, 0:B_W])
    ig = _sigmoid(gates[:, B_W:])
    lam = lam_ref[...]
    softplus_neg_lam = jnp.maximum(-lam, 0.0) + jnp.log1p(jnp.exp(-jnp.abs(lam)))
    log_a = (-LRU_C) * r * softplus_neg_lam
    a = jnp.exp(log_a)
    u = jnp.sqrt(-jnp.tanh(log_a) * (a * a + 1.0)) * (ig * y)

    row = lax.broadcasted_iota(jnp.int32, (tl, 1), 0)
    step = 1
    while step < tl:
        keep = row >= step
        a_prev = jnp.where(keep, pltpu.roll(a, step, 0), 1.0)
        u_prev = jnp.where(keep, pltpu.roll(u, step, 0), 0.0)
        u = u + a * u_prev
        a = a * a_prev
        step *= 2
    h = u + a * h_ref[...]
    h_ref[...] = h[tl - 1:tl, :]
    o_ref[0] = h


def _rg_lru(zb, conv_w, conv_b, w_gates, b_gates, lam):
    b, s, w = zb.shape
    tl = LRU_TL
    const = lambda shp: pl.BlockSpec(shp, lambda i, j: (0, 0))
    return pl.pallas_call(
        _lru_kernel,
        out_shape=jax.ShapeDtypeStruct((b, s, w), F32),
        grid=(b, s // tl),
        in_specs=[pl.BlockSpec((1, tl, w), lambda i, j: (i, j, 0)),
                  const(conv_w.shape), const((1, w)), const(w_gates.shape), const((1, 2 * w)), const((1, w))],
        out_specs=pl.BlockSpec((1, tl, w), lambda i, j: (i, j, 0)),
        scratch_shapes=[pltpu.VMEM((tl + V7X_SUBLANES, w), F32), pltpu.VMEM((1, w), F32)],
        compiler_params=_params("arbitrary", "arbitrary"),
        name="rg_lru",
    )(zb, conv_w, conv_b.reshape(1, w), w_gates, b_gates.reshape(1, 2 * w), lam.reshape(1, w))


def _dilated_kernel(q_ref, k_ref, v_ref, o_ref, lse_ref, *, n_steps):
    qb = DIL_QBLOCK
    tq = q_ref.shape[1]
    lane_head = lax.broadcasted_iota(jnp.int32, (1, C_W), 1) // HEAD_DIM
    qi = lax.broadcasted_iota(jnp.int32, (qb, 2 * qb), 0)
    kj = lax.broadcasted_iota(jnp.int32, (qb, 2 * qb), 1)
    for sub in range(tq // qb):
        base = pl.program_id(2) * tq + sub * qb
        start = pl.multiple_of(jnp.maximum(base - qb, 0), qb)
        dist = (base - start) + qi - kj
        valid = (dist >= 0) & (dist <= n_steps)
        q = q_ref[0, sub * qb:(sub + 1) * qb, :]
        kk = k_ref[0, pl.ds(start, 2 * qb), :]
        vv = v_ref[0, pl.ds(start, 2 * qb), :]
        out = jnp.zeros((qb, C_W), F32)
        lse = jnp.zeros((qb, C_W), F32)
        for h in range(DIL_HEADS):
            in_head = lane_head == h
            qh = jnp.where(in_head, q, jnp.zeros_like(q))
            s = jnp.where(valid, _dot_nt(qh, kk), NEG)
            m = jnp.max(s, axis=-1, keepdims=True)
            e = jnp.exp(s - m)
            l = jnp.sum(e, axis=-1, keepdims=True)
            pv = jnp.dot(e.astype(BF16), vv, preferred_element_type=F32)
            out = jnp.where(in_head, pv / l, out)
            lse = jnp.where(in_head, m + jnp.log(l), lse)
        o_ref[0, sub * qb:(sub + 1) * qb, :] = out
        lse_ref[0, sub * qb:(sub + 1) * qb, :] = lse


def _dilated_branch(cq, ck, cv, window, dil):
    b, s, w = cq.shape
    length = s // dil
    tq = min(DIL_TQ, length)
    view = lambda t: t.reshape(b, length, dil * w)
    qspec = pl.BlockSpec((1, tq, w), lambda i, r, j: (i, j, r))
    kvspec = pl.BlockSpec((1, length, w), lambda i, r, j: (i, 0, r))
    o, lse = pl.pallas_call(
        functools.partial(_dilated_kernel, n_steps=window // dil),
        out_shape=(jax.ShapeDtypeStruct((b, length, dil * w), F32),) * 2,
        grid=(b, dil, length // tq),
        in_specs=[qspec, kvspec, kvspec],
        out_specs=(qspec, qspec),
        compiler_params=_params("parallel", "parallel", "parallel"),
        name=f"dilated_attention_d{dil}",
    )(view(cq), view(ck), view(cv))
    return o.reshape(b, s, w), lse.reshape(b, s, w)


def _dil_merge_kernel(o1, l1, o2, l2, o3, l3, y_ref):
    la, lb, lc = l1[0], l2[0], l3[0]
    m = jnp.maximum(jnp.maximum(la, lb), lc)
    wa, wb, wc = jnp.exp(la - m), jnp.exp(lb - m), jnp.exp(lc - m)
    y_ref[0] = (wa * o1[0] + wb * o2[0] + wc * o3[0]) / (wa + wb + wc)


def _dilated(cq, ck, cv):
    b, s, w = cq.shape
    parts = []
    for window, dil in DIL_CONFIGS:
        parts.extend(_dilated_branch(cq, ck, cv, window, dil))
    tm = OUT_TM
    spec = pl.BlockSpec((1, tm, w), lambda i, j: (i, j, 0))
    return pl.pallas_call(
        _dil_merge_kernel,
        out_shape=jax.ShapeDtypeStruct((b, s, w), F32),
        grid=(b, s // tm),
        in_specs=[spec] * 6,
        out_specs=spec,
        compiler_params=_params("parallel", "parallel"),
        name="dilated_merge",
    )(*parts)


def _gla_kernel(z_ref, gn_ref, o_ref, state_ref, b_ref, q_ref, t_ref, y_ref):
    c = GLA_CHUNK
    tg = z_ref.shape[1]
    i = pl.program_id(1)

    @pl.when(i == 0)
    def _():
        state_ref[...] = jnp.zeros_like(state_ref)

    qs = z_ref[0, :, 0:D_KW]
    ks = z_ref[0, :, D_KW:2 * D_KW]
    vs = z_ref[0, :, 2 * D_KW:2 * D_KW + D_W]
    la = z_ref[0, :, 2 * D_KW + D_W:]

    ri = lax.broadcasted_iota(jnp.int32, (tg, tg), 0)
    ci = lax.broadcasted_iota(jnp.int32, (tg, tg), 1)
    tri = ((ri // c == ci // c) & (ci <= ri)).astype(F32)
    b_ref[...] = jnp.dot(tri, la, preferred_element_type=F32, precision=HIGHEST)
    q_ref[...] = qs

    dk_head = lax.broadcasted_iota(jnp.int32, (D_KW, D_W), 0) // GLA_DK
    dv_head = lax.broadcasted_iota(jnp.int32, (D_KW, D_W), 1) // GLA_DV
    same_head = dk_head == dv_head
    head_sum = same_head.astype(BF16)
    eye = (lax.broadcasted_iota(jnp.int32, (D_KW, D_KW), 0) == lax.broadcasted_iota(jnp.int32, (D_KW, D_KW), 1))
    jrow = lax.broadcasted_iota(jnp.int32, (c, D_KW), 0)

    for ch in range(tg // c):
        r0 = ch * c
        bch = b_ref[r0:r0 + c, :]
        kch = ks[r0:r0 + c, :]
        vch = vs[r0:r0 + c, :]
        b_last = bch[c - 1:c, :]
        state = state_ref[...]
        q_in = qs[r0:r0 + c, :] * jnp.exp(bch)
        o_inter = jnp.dot(q_in, state, preferred_element_type=F32, precision=HIGHEST)
        k_dec = kch * jnp.exp(b_last - bch)
        kv = lax.dot_general(k_dec, vch, (((0,), (0,)), ((), ())), preferred_element_type=F32, precision=HIGHEST)
        dmat = jnp.where(eye, jnp.exp(b_last), 0.0)
        state_ref[...] = (jnp.dot(dmat, state, preferred_element_type=F32, precision=HIGHEST)
                          + jnp.where(same_head, kv, 0.0))

        def row_terms(ii, carry):
            bi = b_ref[pl.ds(r0 + ii, 1), :]
            qi = q_ref[pl.ds(r0 + ii, 1), :]
            t = jnp.exp(jnp.minimum(bi - bch, 0.0)) * kch * qi
            t_ref[pl.ds(pl.multiple_of(ii * c, c), c), :] = jnp.where(jrow <= ii, t, 0.0)
            return carry

        lax.fori_loop(0, c, row_terms, 0)
        a_b = jnp.dot(t_ref[...].astype(BF16), head_sum, preferred_element_type=F32)
        o_intra = jnp.sum(a_b.reshape(c, c, D_W) * vch[None, :, :], axis=1)
        y_ref[r0:r0 + c, :] = o_intra + o_inter

    y = y_ref[...]
    e_head = (lax.broadcasted_iota(jnp.int32, (D_W, D_W), 0) // GLA_DV
              == lax.broadcasted_iota(jnp.int32, (D_W, D_W), 1) // GLA_DV).astype(F32) * (1.0 / GLA_DV)
    ms = jnp.dot(y * y, e_head, preferred_element_type=F32, precision=HIGHEST)
    o_ref[0] = y * lax.rsqrt(ms + LN_EPS) * gn_ref[...]


def _gla(zd, gla_gn):
    b, s, w = zd.shape
    tg = GLA_TG
    gn_row = jnp.tile(gla_gn.astype(F32), GLA_HEADS).reshape(1, D_W)
    return pl.pallas_call(
        _gla_kernel,
        out_shape=jax.ShapeDtypeStruct((b, s, D_W), F32),
        grid=(b, s // tg),
        in_specs=[pl.BlockSpec((1, tg, w), lambda i, j: (i, j, 0)),
                  pl.BlockSpec((1, D_W), lambda i, j: (0, 0))],
        out_specs=pl.BlockSpec((1, tg, D_W), lambda i, j: (i, j, 0)),
        scratch_shapes=[pltpu.VMEM((D_KW, D_W), F32),
                        pltpu.VMEM((tg, D_KW), F32),
                        pltpu.VMEM((tg, D_KW), F32),
                        pltpu.VMEM((GLA_CHUNK * GLA_CHUNK, D_KW), F32),
                        pltpu.VMEM((tg, D_W), F32)],
        compiler_params=_params("arbitrary", "arbitrary"),
        name="gla",
    )(zd, gn_row)


def _outproj_kernel(ya_ref, yb_ref, yc_ref, yd_ref, zg_ref, x_ref, mod_ref, w_ref, g_ref, b_ref, o_ref, *, alpha):
    d = x_ref.shape[2]
    y = jnp.zeros((x_ref.shape[1], d), F32)
    for j, y_ref in enumerate((ya_ref, yb_ref, yc_ref, yd_ref)):
        mix = (y_ref[0] * zg_ref[0, :, j * A_W:(j + 1) * A_W]).astype(BF16)
        y = y + jnp.dot(mix, w_ref[j * A_W:(j + 1) * A_W, :], preferred_element_type=F32)
    gate = mod_ref[0, :, 2 * d:3 * d]
    r = alpha * x_ref[0] + (1.0 + gate) * y
    mu = jnp.mean(r, axis=-1, keepdims=True)
    rc = r - mu
    var = jnp.mean(rc * rc, axis=-1, keepdims=True)
    o_ref[0] = rc * lax.rsqrt(var + LN_EPS) * g_ref[...] + b_ref[...]


def _out_projection(ya, yb, yc, yd, zg, x, mod_l, w_out_bf, ln_g, ln_b, alpha):
    b, s, d = x.shape
    tm = OUT_TM
    row3 = lambda w: pl.BlockSpec((1, tm, w), lambda i, j: (i, j, 0))
    const2 = lambda shp: pl.BlockSpec(shp, lambda i, j: (0, 0))
    return pl.pallas_call(
        functools.partial(_outproj_kernel, alpha=alpha),
        out_shape=jax.ShapeDtypeStruct((b, s, d), F32),
        grid=(b, s // tm),
        in_specs=[row3(A_W), row3(B_W), row3(C_W), row3(D_W), row3(4 * A_W), row3(d),
                  pl.BlockSpec((1, 1, mod_l.shape[-1]), lambda i, j: (i, 0, 0)),
                  const2(w_out_bf.shape), const2((1, d)), const2((1, d))],
        out_specs=row3(d),
        compiler_params=_params("parallel", "parallel"),
        name="out_projection",
    )(ya, yb, yc, yd, zg, x, mod_l.reshape(b, 1, -1), w_out_bf, ln_g.reshape(1, d), ln_b.reshape(1, d))


def _block_diag(w):
    g, bd, _ = w.shape
    eye = jnp.eye(g, dtype=w.dtype)
    return (eye[:, None, :, None] * w[:, :, None, :]).reshape(g * bd, g * bd)


def kernel(x, c, positions, w_mod, b_mod, w_in, conv_w, conv_b, lru_wa, lru_ba, lru_wx, lru_bx, lru_lam,
           gla_wr, gla_br, gla_gn, w_out, ln_g, ln_b):
    depth = w_mod.shape[0]
    alpha = (2 * depth) ** 0.25
    mod = _modulation(c, w_mod, b_mod)
    cos_t, sin_t = _rope_tables(positions)
    for l in range(depth):
        w_in_bf = w_in[l].astype(BF16)
        wvt_bf = w_in[l][:, OFF_AV:OFF_AV + A_W].T.astype(BF16)
        aqk, avt, cq, ck, cv, zg, zb, zd = _in_projection(x, mod[l], w_in_bf, wvt_bf, cos_t, sin_t,
                                                           gla_wr[l], gla_br[l])
        ya = _moba(aqk, avt)
        w_gates = jnp.concatenate([_block_diag(lru_wa[l]), _block_diag(lru_wx[l])], axis=1)
        b_gates = jnp.concatenate([lru_ba[l], lru_bx[l]])
        yb = _rg_lru(zb, conv_w[l], conv_b[l], w_gates, b_gates, lru_lam[l])
        yc = _dilated(cq, ck, cv)
        yd = _gla(zd, gla_gn[l])
        x = _out_projection(ya, yb, yc, yd, zg, x, mod[l], w_out[l].astype(BF16), ln_g[l], ln_b[l], alpha)
    return x
```

```python
import functools

import jax
import jax.numpy as jnp
from jax import lax
from jax.experimental import pallas as pl
from jax.experimental.pallas import tpu as pltpu

F32 = jnp.float32
BF16 = jnp.bfloat16
HIGHEST = lax.Precision.HIGHEST

HEAD_DIM = 64
MOBA_HEADS = 4
MOBA_BLOCK = 256
MOBA_TOPK = 3
LRU_WIDTH = 256
LRU_BLOCKS = 4
CONV_WIDTH = 4
LRU_C = 8.0
DIL_HEADS = 4
DIL_CONFIGS = ((128, 1), (512, 4), (2048, 16))
DIL_QBLOCK = 128
GLA_HEADS = 4
GLA_DK = 32
GLA_DV = 64
GLA_LOWRANK = 16
GLA_TAU = 16.0
GLA_CHUNK = 32
ROPE_THETA = 10000.0
LN_EPS = 1e-5
NEG = -1e30
LOG2E = 1.4426950408889634
MOBA_VROWS = HEAD_DIM + 16
GLA_SAFE_DECAY = 60.0

A_W = MOBA_HEADS * HEAD_DIM
B_W = LRU_WIDTH
C_W = DIL_HEADS * HEAD_DIM
D_KW = GLA_HEADS * GLA_DK
D_W = GLA_HEADS * GLA_DV
OFF_AQ, OFF_AK, OFF_AV, OFF_AG = 0, A_W, 2 * A_W, 3 * A_W
OFF_BX = 4 * A_W
OFF_BG = OFF_BX + B_W
OFF_CQ = OFF_BG + B_W
OFF_CK, OFF_CV, OFF_CG = OFF_CQ + C_W, OFF_CQ + 2 * C_W, OFF_CQ + 3 * C_W
OFF_DQ = OFF_CQ + 4 * C_W
OFF_DK = OFF_DQ + D_KW
OFF_DV = OFF_DK + D_KW
OFF_DG = OFF_DV + D_W
OFF_DR = OFF_DG + D_W
D_IN = OFF_DR + GLA_LOWRANK

V7X_LANES = 128
V7X_SUBLANES = 8
V7X_VMEM_LIMIT_BYTES = 48 * 1024 * 1024

IN_TM = 512
OUT_TM = 512
LRU_TL = 512
GLA_TG = 256
ROPE_TR = 512
DIL_TILE = DIL_QBLOCK * max(d for _, d in DIL_CONFIGS)


def _params(*sem):
    return pltpu.CompilerParams(dimension_semantics=sem, vmem_limit_bytes=V7X_VMEM_LIMIT_BYTES)


def _sigmoid(x):
    return 1.0 / (1.0 + jnp.exp(-x))


def _log_sigmoid(x):
    return jnp.minimum(x, 0.0) - jnp.log1p(jnp.exp(-jnp.abs(x)))


def _dot_nt(a, b, **kw):
    return lax.dot_general(a, b, (((1,), (1,)), ((), ())), preferred_element_type=F32, **kw)


def _mod_kernel(c_ref, w_ref, b_ref, o_ref):
    o_ref[0] = jnp.dot(c_ref[...], w_ref[0], preferred_element_type=F32, precision=HIGHEST) + b_ref[0]


def _modulation(c, w_mod, b_mod):
    depth, d, d3 = w_mod.shape
    b = c.shape[0]
    nt = d3 // d
    return pl.pallas_call(
        _mod_kernel,
        out_shape=jax.ShapeDtypeStruct((depth, b, d3), F32),
        grid=(depth, nt),
        in_specs=[pl.BlockSpec((b, d), lambda l, j: (0, 0)),
                  pl.BlockSpec((1, d, d), lambda l, j: (l, 0, j)),
                  pl.BlockSpec((1, 1, d), lambda l, j: (l, 0, j))],
        out_specs=pl.BlockSpec((1, b, d), lambda l, j: (l, 0, j)),
        compiler_params=_params("parallel", "parallel"),
        name="adaln_modulation",
    )(c, w_mod, b_mod.reshape(depth, 1, d3))


def _rope_table_kernel(pos_ref, inv_ref, sgn_ref, cos_ref, sin_ref):
    ang = pos_ref[0].astype(F32) * inv_ref[...]
    cos_ref[0] = jnp.cos(ang)
    sin_ref[0] = jnp.sin(ang) * sgn_ref[...]


def _rope_tables(positions):
    b, s = positions.shape
    half = HEAD_DIM // 2
    inv = ROPE_THETA ** (-jnp.arange(half, dtype=F32) / half)
    reps = V7X_LANES // half
    inv_row = jnp.tile(inv, reps).reshape(1, V7X_LANES)
    sgn_row = jnp.tile(jnp.concatenate([-jnp.ones(half, F32), jnp.ones(half, F32)]), reps // 2).reshape(1, V7X_LANES)
    tab = jax.ShapeDtypeStruct((b, s, V7X_LANES), F32)
    row = pl.BlockSpec((1, V7X_LANES), lambda i, j: (0, 0))
    out = pl.BlockSpec((1, ROPE_TR, V7X_LANES), lambda i, j: (i, j, 0))
    return pl.pallas_call(
        _rope_table_kernel,
        out_shape=(tab, tab),
        grid=(b, s // ROPE_TR),
        in_specs=[pl.BlockSpec((1, ROPE_TR, 1), lambda i, j: (i, j, 0)), row, row],
        out_specs=(out, out),
        compiler_params=_params("parallel", "parallel"),
        name="rope_tables",
    )(positions.reshape(b, s, 1), inv_row, sgn_row)


def _rope(z, cos, sin):
    lane = lax.broadcasted_iota(jnp.int32, (1, V7X_LANES), 1)
    first_half = (lane % HEAD_DIM) < (HEAD_DIM // 2)
    outs = []
    for s in range(z.shape[1] // V7X_LANES):
        zs = z[:, s * V7X_LANES:(s + 1) * V7X_LANES]
        swapped = jnp.where(first_half, pltpu.roll(zs, V7X_LANES - HEAD_DIM // 2, 1), pltpu.roll(zs, HEAD_DIM // 2, 1))
        outs.append(zs * cos + swapped * sin)
    return jnp.concatenate(outs, axis=-1)


def _inproj_kernel(x_ref, mod_ref, w_ref, wvt_ref, cos_ref, sin_ref, wr_ref, br_ref,
                   aqk_ref, avt_ref, cq_ref, ck_ref, cv_ref, zg_ref, zb_ref, zd_ref):
    d = x_ref.shape[2]
    x = x_ref[0]
    mu = jnp.mean(x, axis=-1, keepdims=True)
    xc = x - mu
    var = jnp.mean(xc * xc, axis=-1, keepdims=True)
    xn = xc * lax.rsqrt(var + LN_EPS)
    shift = mod_ref[0, :, 0:d]
    scale = mod_ref[0, :, d:2 * d]
    u = (xn * (1.0 + scale) + shift).astype(BF16)

    def proj(off, width):
        return jnp.dot(u, w_ref[:, off:off + width], preferred_element_type=F32)

    cos = cos_ref[0]
    sin = sin_ref[0]
    q_scale = HEAD_DIM ** -0.5
    aqk_ref[0, :, 0:A_W] = (_rope(proj(OFF_AQ, A_W), cos, sin) * (q_scale * LOG2E)).astype(BF16)
    aqk_ref[0, :, A_W:2 * A_W] = _rope(proj(OFF_AK, A_W), cos, sin).astype(BF16)
    vt = _dot_nt(wvt_ref[...], u).astype(BF16)
    ones = jnp.ones((MOBA_VROWS - HEAD_DIM, vt.shape[1]), BF16)
    for h in range(MOBA_HEADS):
        avt_ref[0, h * MOBA_VROWS:h * MOBA_VROWS + HEAD_DIM, :] = vt[h * HEAD_DIM:(h + 1) * HEAD_DIM, :]
        avt_ref[0, h * MOBA_VROWS + HEAD_DIM:(h + 1) * MOBA_VROWS, :] = ones
    for ref, z in ((cq_ref, _rope(proj(OFF_CQ, C_W), cos, sin) * (q_scale * LOG2E)),
                   (ck_ref, _rope(proj(OFF_CK, C_W), cos, sin)),
                   (cv_ref, proj(OFF_CV, C_W))):
        for half in range(C_W // V7X_LANES):
            ref[0, half] = z[:, half * V7X_LANES:(half + 1) * V7X_LANES]
    for j, off in enumerate((OFF_AG, OFF_BG, OFF_CG, OFF_DG)):
        g = proj(off, A_W)
        zg_ref[0, :, j * A_W:(j + 1) * A_W] = g * _sigmoid(g)
    zb_ref[0] = proj(OFF_BX, B_W)
    zd_ref[0, :, 0:D_KW] = proj(OFF_DQ, D_KW) * (GLA_DK ** -0.5)
    zd_ref[0, :, D_KW:2 * D_KW] = proj(OFF_DK, D_KW)
    zd_ref[0, :, 2 * D_KW:2 * D_KW + D_W] = proj(OFF_DV, D_W)
    dr = proj(OFF_DR, GLA_LOWRANK)
    la = jnp.dot(dr, wr_ref[...], preferred_element_type=F32, precision=HIGHEST) + br_ref[...]
    zd_ref[0, :, 2 * D_KW + D_W:] = _log_sigmoid(la) * (1.0 / GLA_TAU)


def _in_projection(x, mod_l, w_in_bf, wvt_bf, cos_t, sin_t, gla_wr, gla_br):
    b, s, d = x.shape
    tm = IN_TM
    row3 = lambda w: pl.BlockSpec((1, tm, w), lambda i, j: (i, j, 0))
    const2 = lambda shp: pl.BlockSpec(shp, lambda i, j: (0, 0))
    halves = pl.BlockSpec((1, C_W // V7X_LANES, tm, V7X_LANES), lambda i, j: (i, 0, j, 0))
    out_shape = (
        jax.ShapeDtypeStruct((b, s, 2 * A_W), BF16),
        jax.ShapeDtypeStruct((b, MOBA_HEADS * MOBA_VROWS, s), BF16),
        jax.ShapeDtypeStruct((b, C_W // V7X_LANES, s, V7X_LANES), F32),
        jax.ShapeDtypeStruct((b, C_W // V7X_LANES, s, V7X_LANES), F32),
        jax.ShapeDtypeStruct((b, C_W // V7X_LANES, s, V7X_LANES), F32),
        jax.ShapeDtypeStruct((b, s, 4 * A_W), F32),
        jax.ShapeDtypeStruct((b, s, B_W), F32),
        jax.ShapeDtypeStruct((b, s, 2 * D_KW + D_W + D_KW), F32),
    )
    out_specs = (row3(2 * A_W), pl.BlockSpec((1, MOBA_HEADS * MOBA_VROWS, tm), lambda i, j: (i, 0, j)),
                 halves, halves, halves, row3(4 * A_W), row3(B_W), row3(2 * D_KW + D_W + D_KW))
    return pl.pallas_call(
        _inproj_kernel,
        out_shape=out_shape,
        grid=(b, s // tm),
        in_specs=[row3(d),
                  pl.BlockSpec((1, 1, mod_l.shape[-1]), lambda i, j: (i, 0, 0)),
                  const2(w_in_bf.shape), const2(wvt_bf.shape),
                  row3(V7X_LANES), row3(V7X_LANES),
                  const2(gla_wr.shape), const2((1, D_KW))],
        out_specs=out_specs,
        compiler_params=_params("parallel", "parallel"),
        name="in_projection",
    )(x, mod_l.reshape(b, 1, -1), w_in_bf, wvt_bf, cos_t, sin_t, gla_wr, gla_br.reshape(1, D_KW))


def _moba_kernel(q_ref, k_ref, vt_ref, o_ref, kmean_ref, qm_ref, unsel_ref, s_ref, acc_ref):
    blk = MOBA_BLOCK
    nb = k_ref.shape[1] // blk
    n = pl.program_id(1)

    @pl.when(n == 0)
    def _():
        for j in range(nb):
            kj = k_ref[0, j * blk:(j + 1) * blk, :].astype(F32)
            kmean_ref[j:j + 1, :] = jnp.mean(kj, axis=0, keepdims=True)

    q = q_ref[0]
    lane_head = lax.broadcasted_iota(jnp.int32, (1, A_W), 1) // HEAD_DIM
    km = kmean_ref[...]
    km_hi = km.astype(BF16)
    km_lo = (km - km_hi.astype(F32)).astype(BF16)
    key_idx = lax.broadcasted_iota(jnp.int32, (blk, blk), 0)
    qry_idx = lax.broadcasted_iota(jnp.int32, (blk, blk), 1)
    causal = key_idx <= qry_idx

    vr = MOBA_VROWS
    vth = lambda h, j: vt_ref[0, h * vr:(h + 1) * vr, pl.ds(pl.multiple_of(j * blk, blk), blk)]
    kblk = lambda j: k_ref[0, pl.ds(pl.multiple_of(j * blk, blk), blk), :]

    for h in range(MOBA_HEADS):
        qm_ref[h * blk:(h + 1) * blk, :] = jnp.where(lane_head == h, q, jnp.zeros_like(q))
    scores = lambda j: _dot_nt(kblk(j), qm_ref[...])
    s_own = scores(n)
    s_ref[0] = scores(0)

    blk_id = lax.broadcasted_iota(jnp.int32, (nb, MOBA_HEADS * blk), 0)
    gate = _dot_nt(km_hi, qm_ref[...]) + _dot_nt(km_lo, qm_ref[...])
    gate = jnp.where(blk_id < n, gate, NEG)
    rank = jnp.zeros(gate.shape, jnp.int32)
    for i in range(nb):
        gi = gate[i:i + 1, :]
        tie = (blk_id > i).astype(jnp.int32)
        rank = rank + jnp.where(gi > gate, 1, jnp.where(gi == gate, tie, 0))
    unsel_ref[...] = jnp.where((rank < MOBA_TOPK) & (blk_id < n), 0.0, 1.0)

    m_init = []
    for h in range(MOBA_HEADS):
        s = jnp.where(causal, s_own[:, h * blk:(h + 1) * blk], NEG)
        m = jnp.max(s, axis=0, keepdims=True)
        p = jnp.exp2(s - m)
        acc_ref[h * vr:(h + 1) * vr, :] = jnp.dot(vth(h, n), p.astype(BF16), preferred_element_type=F32)
        m_init.append(m)

    def body(j, ms):
        s_all = s_ref[j % 2]
        s_ref[(j + 1) % 2] = scores(jnp.minimum(j + 1, nb - 1))
        new_ms, ps, alphas = [], [], []
        for h in range(MOBA_HEADS):
            unsel = unsel_ref[pl.ds(j, 1), h * blk:(h + 1) * blk] > 0.5
            s = s_all[:, h * blk:(h + 1) * blk]
            cm = jnp.where(unsel, NEG, jnp.max(s, axis=0, keepdims=True))
            m_new = jnp.maximum(ms[h], cm)
            alphas.append(jnp.exp2(ms[h] - m_new))
            ps.append(jnp.exp2(s - jnp.where(unsel, -NEG, m_new)).astype(BF16))
            new_ms.append(m_new)
        for h in range(MOBA_HEADS):
            acc_ref[h * vr:(h + 1) * vr, :] = (alphas[h] * acc_ref[h * vr:(h + 1) * vr, :]
                                              + jnp.dot(vth(h, j), ps[h], preferred_element_type=F32))
        return tuple(new_ms)

    lax.fori_loop(0, n, body, tuple(m_init))
    outs = []
    for h in range(MOBA_HEADS):
        a = acc_ref[h * vr:(h + 1) * vr, :]
        outs.append(a[0:HEAD_DIM, :] / a[HEAD_DIM:HEAD_DIM + 1, :])
    o_ref[0] = jnp.concatenate(outs, axis=0).T


def _moba(aqk, avt):
    b, s, _ = aqk.shape
    blk = MOBA_BLOCK
    return pl.pallas_call(
        _moba_kernel,
        out_shape=jax.ShapeDtypeStruct((b, s, A_W), F32),
        grid=(b, s // blk),
        in_specs=[pl.BlockSpec((1, blk, A_W), lambda i, n: (i, n, 0)),
                  pl.BlockSpec((1, s, A_W), lambda i, n: (i, 0, 1)),
                  pl.BlockSpec((1, MOBA_HEADS * MOBA_VROWS, s), lambda i, n: (i, 0, 0))],
        out_specs=pl.BlockSpec((1, blk, A_W), lambda i, n: (i, n, 0)),
        scratch_shapes=[pltpu.VMEM((s // blk, A_W), F32),
                        pltpu.VMEM((MOBA_HEADS * blk, A_W), BF16),
                        pltpu.VMEM((s // blk, MOBA_HEADS * blk), F32),
                        pltpu.VMEM((2, blk, MOBA_HEADS * blk), F32),
                        pltpu.VMEM((MOBA_HEADS * MOBA_VROWS, blk), F32)],
        compiler_params=_params("arbitrary", "arbitrary"),
        name="moba_attention",
    )(aqk, aqk, avt)


def _lru_kernel(x_ref, cw_ref, cb_ref, wg_ref, bg_ref, lam_ref, o_ref, xs_ref, h_ref):
    tl = x_ref.shape[1]
    halo = V7X_SUBLANES
    i = pl.program_id(1)

    @pl.when(i == 0)
    def _():
        xs_ref[0:halo, :] = jnp.zeros((halo, B_W), F32)
        h_ref[...] = jnp.zeros_like(h_ref)

    x = x_ref[0]
    xs_ref[halo:, :] = x
    y = cb_ref[...] + cw_ref[CONV_WIDTH - 1:CONV_WIDTH, :] * x
    for k in range(CONV_WIDTH - 1):
        y = y + cw_ref[k:k + 1, :] * xs_ref[pl.ds(halo - (CONV_WIDTH - 1) + k, tl), :]
    xs_ref[0:halo, :] = x[tl - halo:, :]

    gates = jnp.dot(y, wg_ref[...], preferred_element_type=F32, precision=HIGHEST) + bg_ref[...]
    r = _sigmoid(gates[:, 0:B_W])
    ig = _sigmoid(gates[:, B_W:])
    lam = lam_ref[...]
    softplus_neg_lam = jnp.maximum(-lam, 0.0) + jnp.log1p(jnp.exp(-jnp.abs(lam)))
    log_a = (-LRU_C) * r * softplus_neg_lam
    a = jnp.exp(log_a)
    u = jnp.sqrt(-jnp.tanh(log_a) * (a * a + 1.0)) * (ig * y)

    row = lax.broadcasted_iota(jnp.int32, (tl, 1), 0)
    step = 1
    while step < tl:
        keep = row >= step
        a_prev = jnp.where(keep, pltpu.roll(a, step, 0), 1.0)
        u_prev = jnp.where(keep, pltpu.roll(u, step, 0), 0.0)
        u = u + a * u_prev
        a = a * a_prev
        step *= 2
    h = u + a * h_ref[...]
    h_ref[...] = h[tl - 1:tl, :]
    o_ref[0] = h


def _rg_lru(zb, conv_w, conv_b, w_gates, b_gates, lam):
    b, s, w = zb.shape
    tl = LRU_TL
    const = lambda shp: pl.BlockSpec(shp, lambda i, j: (0, 0))
    return pl.pallas_call(
        _lru_kernel,
        out_shape=jax.ShapeDtypeStruct((b, s, w), F32),
        grid=(b, s // tl),
        in_specs=[pl.BlockSpec((1, tl, w), lambda i, j: (i, j, 0)),
                  const(conv_w.shape), const((1, w)), const(w_gates.shape), const((1, 2 * w)), const((1, w))],
        out_specs=pl.BlockSpec((1, tl, w), lambda i, j: (i, j, 0)),
        scratch_shapes=[pltpu.VMEM((tl + V7X_SUBLANES, w), F32), pltpu.VMEM((1, w), F32)],
        compiler_params=_params("arbitrary", "arbitrary"),
        name="rg_lru",
    )(zb, conv_w, conv_b.reshape(1, w), w_gates, b_gates.reshape(1, 2 * w), lam.reshape(1, w))


def _dilated_kernel(q_ref, k_ref, v_ref, o_ref, oacc_ref, lacc_ref):
    qb = DIL_QBLOCK
    tile = o_ref.shape[1]
    t0 = pl.program_id(1) * tile
    halves = range(C_W // V7X_LANES)

    def load(ref, rows):
        return jnp.concatenate([ref.at[half][rows, :] for half in halves], axis=-1)

    def store(ref, rows, value):
        for half in halves:
            ref.at[half][rows, :] = value[:, half * V7X_LANES:(half + 1) * V7X_LANES]

    lane_head = lax.broadcasted_iota(jnp.int32, (1, C_W), 1) // HEAD_DIM
    qi = lax.broadcasted_iota(jnp.int32, (qb, 2 * qb), 0)
    kj = lax.broadcasted_iota(jnp.int32, (qb, 2 * qb), 1)

    for ci, (window, d) in enumerate(DIL_CONFIGS):
        n_steps = window // d
        per_class = tile // (qb * d)
        first, last = ci == 0, ci == len(DIL_CONFIGS) - 1

        def unit(u, carry, d=d, n_steps=n_steps, per_class=per_class, first=first, last=last):
            r = u // per_class
            base = t0 // d + (u % per_class) * qb
            start = jnp.maximum(base - qb, 0)
            if d == 1:
                tokens = lambda m0, count: pl.ds(pl.multiple_of(m0, qb), count)
            else:
                tokens = lambda m0, count: pl.ds(m0 * d + r, count, stride=d)
            q = load(q_ref, tokens(base, qb))
            kk = load(k_ref, tokens(start, 2 * qb)).astype(BF16)
            vv = load(v_ref, tokens(start, 2 * qb)).astype(BF16)
            dist = (base - start) + qi - kj
            valid = (dist >= 0) & (dist <= n_steps)
            q_stack = jnp.concatenate([jnp.where(lane_head == h, q, 0.0) for h in range(DIL_HEADS)], axis=0)
            s = _dot_nt(q_stack.astype(BF16), kk)
            s = jnp.where(jnp.concatenate([valid] * DIL_HEADS, axis=0), s, NEG)
            m = jnp.max(s, axis=-1, keepdims=True)
            e = jnp.exp2(s - m)
            l = jnp.sum(e, axis=-1, keepdims=True)
            pv = jnp.dot(e.astype(BF16), vv, preferred_element_type=F32)
            o_all = pv / l
            lse_all = m + jnp.log2(l)
            out = o_all[0:qb, :]
            lse = jnp.broadcast_to(lse_all[0:qb, :], (qb, C_W))
            for h in range(1, DIL_HEADS):
                out = jnp.where(lane_head == h, o_all[h * qb:(h + 1) * qb, :], out)
                lse = jnp.where(lane_head == h, lse_all[h * qb:(h + 1) * qb, :], lse)
            local = tokens(base - t0 // d, qb)
            if not first:
                o_prev = load(oacc_ref, local)
                l_prev = load(lacc_ref, local)
                top = jnp.maximum(l_prev, lse)
                w_prev = jnp.exp2(l_prev - top)
                w_cur = jnp.exp2(lse - top)
                out = (w_prev * o_prev + w_cur * out) / (w_prev + w_cur)
                lse = top + jnp.log2(w_prev + w_cur)
            if last:
                store(o_ref, local, out)
            else:
                store(oacc_ref, local, out)
                store(lacc_ref, local, lse)
            return carry

        lax.fori_loop(0, tile // qb, unit, 0)


def _dilated(cq, ck, cv):
    b, nh, s, w = cq.shape
    tile = DIL_TILE
    whole = pl.BlockSpec((None, nh, s, w), lambda i, j: (i, 0, 0, 0))
    return pl.pallas_call(
        _dilated_kernel,
        out_shape=jax.ShapeDtypeStruct((b, nh, s, w), F32),
        grid=(b, s // tile),
        in_specs=[whole, whole, whole],
        out_specs=pl.BlockSpec((None, nh, tile, w), lambda i, j: (i, 0, j, 0)),
        scratch_shapes=[pltpu.VMEM((nh, tile, w), F32), pltpu.VMEM((nh, tile, w), F32)],
        compiler_params=_params("parallel", "arbitrary"),
        name="dilated_attention",
    )(cq, ck, cv)


def _gla_kernel(z_ref, gn_ref, o_ref, state_ref, b_ref, q_ref, k_ref, v_ref, t_ref, y_ref):
    c = GLA_CHUNK
    tg = z_ref.shape[1]
    nch = tg // c
    i = pl.program_id(1)

    @pl.when(i == 0)
    def _():
        state_ref[...] = jnp.zeros_like(state_ref)

    qs = z_ref[0, :, 0:D_KW]
    ks = z_ref[0, :, D_KW:2 * D_KW]
    vs = z_ref[0, :, 2 * D_KW:2 * D_KW + D_W]
    la = z_ref[0, :, 2 * D_KW + D_W:]

    ri = lax.broadcasted_iota(jnp.int32, (tg, tg), 0)
    ci = lax.broadcasted_iota(jnp.int32, (tg, tg), 1)
    chunk_causal = (ri // c == ci // c) & (ci <= ri)
    b = jnp.dot(chunk_causal.astype(F32), la, preferred_element_type=F32, precision=HIGHEST)
    b_last = jnp.broadcast_to(b.reshape(nch, c, D_KW)[:, c - 1:c, :], (nch, c, D_KW)).reshape(tg, D_KW)
    q_in = qs * jnp.exp(b)
    k_dec = ks * jnp.exp(b_last - b)
    v_bf = vs.astype(BF16)

    safe = jnp.min(b) > -GLA_SAFE_DECAY

    @pl.when(safe)
    def _():
        k_grow = (ks * jnp.exp(-b)).astype(BF16)
        dk_head = lax.broadcasted_iota(jnp.int32, (1, D_KW), 1) // GLA_DK
        q_stack = jnp.concatenate([jnp.where(dk_head == h, q_in, 0.0) for h in range(GLA_HEADS)], axis=0)
        a = _dot_nt(q_stack.astype(BF16), k_grow)
        a = jnp.where(jnp.concatenate([chunk_causal] * GLA_HEADS, axis=0), a, 0.0)
        av = jnp.dot(a.astype(BF16), v_bf, preferred_element_type=F32)
        dv_head = lax.broadcasted_iota(jnp.int32, (1, D_W), 1) // GLA_DV
        y = av[0:tg, :]
        for h in range(1, GLA_HEADS):
            y = jnp.where(dv_head == h, av[h * tg:(h + 1) * tg, :], y)
        y_ref[...] = y

    @pl.when(jnp.logical_not(safe))
    def _():
        b_ref[...] = b
        q_ref[...] = qs
        k_ref[...] = ks
        v_ref[...] = vs
        head_sum = (lax.broadcasted_iota(jnp.int32, (D_KW, D_W), 0) // GLA_DK
                    == lax.broadcasted_iota(jnp.int32, (D_KW, D_W), 1) // GLA_DV).astype(BF16)
        jrow = lax.broadcasted_iota(jnp.int32, (c, D_KW), 0)

        def chunk(ch, carry):
            r0 = pl.multiple_of(ch * c, c)
            bch = b_ref[pl.ds(r0, c), :]
            kch = k_ref[pl.ds(r0, c), :]

            def row_terms(ii, carry2):
                bi = b_ref[pl.ds(r0 + ii, 1), :]
                qi = q_ref[pl.ds(r0 + ii, 1), :]
                t = jnp.exp(jnp.minimum(bi - bch, 0.0)) * kch * qi
                t_ref[pl.ds(pl.multiple_of(ii * c, c), c), :] = jnp.where(jrow <= ii, t, 0.0)
                return carry2

            lax.fori_loop(0, c, row_terms, 0)
            a_b = jnp.dot(t_ref[...].astype(BF16), head_sum, preferred_element_type=F32)
            y_ref[pl.ds(r0, c), :] = jnp.sum(a_b.reshape(c, c, D_W) * v_ref[pl.ds(r0, c), :][None, :, :], axis=1)
            return carry

        lax.fori_loop(0, nch, chunk, 0)

    same_head_t = (lax.broadcasted_iota(jnp.int32, (D_W, D_KW), 0) // GLA_DV
                   == lax.broadcasted_iota(jnp.int32, (D_W, D_KW), 1) // GLA_DK)
    q_in_bf = q_in.astype(BF16)
    k_dec_bf = k_dec.astype(BF16)
    st = state_ref[...]
    inter = []
    for ch in range(nch):
        r0 = ch * c
        inter.append(_dot_nt(q_in_bf[r0:r0 + c, :], st.astype(BF16)))
        kvt = lax.dot_general(v_bf[r0:r0 + c, :], k_dec_bf[r0:r0 + c, :], (((0,), (0,)), ((), ())),
                              preferred_element_type=F32)
        st = st * jnp.exp(b_last[r0:r0 + 1, :]) + jnp.where(same_head_t, kvt, 0.0)
    state_ref[...] = st
    y = y_ref[...] + jnp.concatenate(inter, axis=0)

    e_head = jnp.where(lax.broadcasted_iota(jnp.int32, (D_W, D_W), 0) // GLA_DV
                       == lax.broadcasted_iota(jnp.int32, (D_W, D_W), 1) // GLA_DV, 1.0 / GLA_DV, 0.0).astype(BF16)
    yy = y * y
    yy_hi = yy.astype(BF16)
    yy_lo = (yy - yy_hi.astype(F32)).astype(BF16)
    ms = (jnp.dot(yy_hi, e_head, preferred_element_type=F32) + jnp.dot(yy_lo, e_head, preferred_element_type=F32))
    o_ref[0] = y * lax.rsqrt(ms + LN_EPS) * gn_ref[...]


def _gla(zd, gla_gn):
    b, s, w = zd.shape
    tg = GLA_TG
    gn_row = jnp.tile(gla_gn.astype(F32), GLA_HEADS).reshape(1, D_W)
    return pl.pallas_call(
        _gla_kernel,
        out_shape=jax.ShapeDtypeStruct((b, s, D_W), F32),
        grid=(b, s // tg),
        in_specs=[pl.BlockSpec((1, tg, w), lambda i, j: (i, j, 0)),
                  pl.BlockSpec((1, D_W), lambda i, j: (0, 0))],
        out_specs=pl.BlockSpec((1, tg, D_W), lambda i, j: (i, j, 0)),
        scratch_shapes=[pltpu.VMEM((D_W, D_KW), F32),
                        pltpu.VMEM((tg, D_KW), F32),
                        pltpu.VMEM((tg, D_KW), F32),
                        pltpu.VMEM((tg, D_KW), F32),
                        pltpu.VMEM((tg, D_W), F32),
                        pltpu.VMEM((GLA_CHUNK * GLA_CHUNK, D_KW), F32),
                        pltpu.VMEM((tg, D_W), F32)],
        compiler_params=_params("arbitrary", "arbitrary"),
        name="gla",
    )(zd, gn_row)


def _outproj_kernel(ya_ref, yb_ref, yc_ref, yd_ref, zg_ref, x_ref, mod_ref, w_ref, g_ref, b_ref, o_ref, *, alpha):
    d = x_ref.shape[2]
    y = jnp.zeros((x_ref.shape[1], d), F32)
    yc = jnp.concatenate([yc_ref[0, half] for half in range(yc_ref.shape[1])], axis=-1)
    for j, yj in enumerate((ya_ref[0], yb_ref[0], yc, yd_ref[0])):
        mix = (yj * zg_ref[0, :, j * A_W:(j + 1) * A_W]).astype(BF16)
        y = y + jnp.dot(mix, w_ref[j * A_W:(j + 1) * A_W, :], preferred_element_type=F32)
    gate = mod_ref[0, :, 2 * d:3 * d]
    r = alpha * x_ref[0] + (1.0 + gate) * y
    mu = jnp.mean(r, axis=-1, keepdims=True)
    rc = r - mu
    var = jnp.mean(rc * rc, axis=-1, keepdims=True)
    o_ref[0] = rc * lax.rsqrt(var + LN_EPS) * g_ref[...] + b_ref[...]


def _out_projection(ya, yb, yc, yd, zg, x, mod_l, w_out_bf, ln_g, ln_b, alpha):
    b, s, d = x.shape
    tm = OUT_TM
    row3 = lambda w: pl.BlockSpec((1, tm, w), lambda i, j: (i, j, 0))
    const2 = lambda shp: pl.BlockSpec(shp, lambda i, j: (0, 0))
    return pl.pallas_call(
        functools.partial(_outproj_kernel, alpha=alpha),
        out_shape=jax.ShapeDtypeStruct((b, s, d), F32),
        grid=(b, s // tm),
        in_specs=[row3(A_W), row3(B_W),
                  pl.BlockSpec((1, yc.shape[1], tm, yc.shape[3]), lambda i, j: (i, 0, j, 0)),
                  row3(D_W), row3(4 * A_W), row3(d),
                  pl.BlockSpec((1, 1, mod_l.shape[-1]), lambda i, j: (i, 0, 0)),
                  const2(w_out_bf.shape), const2((1, d)), const2((1, d))],
        out_specs=row3(d),
        compiler_params=_params("parallel", "parallel"),
        name="out_projection",
    )(ya, yb, yc, yd, zg, x, mod_l.reshape(b, 1, -1), w_out_bf, ln_g.reshape(1, d), ln_b.reshape(1, d))


def _block_diag(w):
    g, bd, _ = w.shape
    eye = jnp.eye(g, dtype=w.dtype)
    return (eye[:, None, :, None] * w[:, :, None, :]).reshape(g * bd, g * bd)


def kernel(x, c, positions, w_mod, b_mod, w_in, conv_w, conv_b, lru_wa, lru_ba, lru_wx, lru_bx, lru_lam,
           gla_wr, gla_br, gla_gn, w_out, ln_g, ln_b):
    depth = w_mod.shape[0]
    alpha = (2 * depth) ** 0.25
    mod = _modulation(c, w_mod, b_mod)
    cos_t, sin_t = _rope_tables(positions)
    for l in range(depth):
        w_in_bf = w_in[l].astype(BF16)
        wvt_bf = w_in[l][:, OFF_AV:OFF_AV + A_W].T.astype(BF16)
        aqk, avt, cq, ck, cv, zg, zb, zd = _in_projection(x, mod[l], w_in_bf, wvt_bf, cos_t, sin_t,
                                                           gla_wr[l], gla_br[l])
        ya = _moba(aqk, avt)
        w_gates = jnp.concatenate([_block_diag(lru_wa[l]), _block_diag(lru_wx[l])], axis=1)
        b_gates = jnp.concatenate([lru_ba[l], lru_bx[l]])
        yb = _rg_lru(zb, conv_w[l], conv_b[l], w_gates, b_gates, lru_lam[l])
        yc = _dilated(cq, ck, cv)
        yd = _gla(zd, gla_gn[l])
        x = _out_projection(ya, yb, yc, yd, zg, x, mod[l], w_out[l].astype(BF16), ln_g[l], ln_b[l], alpha)
    return x
```

```python
import functools

import jax
import jax.numpy as jnp
from jax import lax
from jax.experimental import pallas as pl
from jax.experimental.pallas import tpu as pltpu

F32 = jnp.float32
BF16 = jnp.bfloat16
HIGHEST = lax.Precision.HIGHEST

HEAD_DIM = 64
MOBA_HEADS = 4
MOBA_BLOCK = 256
MOBA_TOPK = 3
LRU_WIDTH = 256
LRU_BLOCKS = 4
CONV_WIDTH = 4
LRU_C = 8.0
DIL_HEADS = 4
DIL_CONFIGS = ((128, 1), (512, 4), (2048, 16))
DIL_QBLOCK = 128
GLA_HEADS = 4
GLA_DK = 32
GLA_DV = 64
GLA_LOWRANK = 16
GLA_TAU = 16.0
GLA_CHUNK = 32
ROPE_THETA = 10000.0
LN_EPS = 1e-5
NEG = -1e30
LOG2E = 1.4426950408889634
MOBA_VROWS = HEAD_DIM + 16
GLA_SAFE_DECAY = 60.0

A_W = MOBA_HEADS * HEAD_DIM
B_W = LRU_WIDTH
C_W = DIL_HEADS * HEAD_DIM
D_KW = GLA_HEADS * GLA_DK
D_W = GLA_HEADS * GLA_DV
OFF_AQ, OFF_AK, OFF_AV, OFF_AG = 0, A_W, 2 * A_W, 3 * A_W
OFF_BX = 4 * A_W
OFF_BG = OFF_BX + B_W
OFF_CQ = OFF_BG + B_W
OFF_CK, OFF_CV, OFF_CG = OFF_CQ + C_W, OFF_CQ + 2 * C_W, OFF_CQ + 3 * C_W
OFF_DQ = OFF_CQ + 4 * C_W
OFF_DK = OFF_DQ + D_KW
OFF_DV = OFF_DK + D_KW
OFF_DG = OFF_DV + D_W
OFF_DR = OFF_DG + D_W
D_IN = OFF_DR + GLA_LOWRANK

V7X_LANES = 128
V7X_SUBLANES = 8
V7X_VMEM_LIMIT_BYTES = 48 * 1024 * 1024

IN_TM = 512
OUT_TM = 512
LRU_TL = 512
GLA_TG = 256
ROPE_TR = 512
DIL_UNROLL = 2
DIL_TILE = DIL_QBLOCK * max(d for _, d in DIL_CONFIGS)


def _params(*sem):
    return pltpu.CompilerParams(dimension_semantics=sem, vmem_limit_bytes=V7X_VMEM_LIMIT_BYTES)


def _sigmoid(x):
    return 1.0 / (1.0 + jnp.exp(-x))


def _log_sigmoid(x):
    return jnp.minimum(x, 0.0) - jnp.log1p(jnp.exp(-jnp.abs(x)))


def _dot_nt(a, b, **kw):
    return lax.dot_general(a, b, (((1,), (1,)), ((), ())), preferred_element_type=F32, **kw)


def _mod_kernel(c_ref, w_ref, b_ref, o_ref):
    o_ref[0] = jnp.dot(c_ref[...], w_ref[0], preferred_element_type=F32, precision=HIGHEST) + b_ref[0]


def _modulation(c, w_mod, b_mod):
    depth, d, d3 = w_mod.shape
    b = c.shape[0]
    nt = d3 // d
    return pl.pallas_call(
        _mod_kernel,
        out_shape=jax.ShapeDtypeStruct((depth, b, d3), F32),
        grid=(depth, nt),
        in_specs=[pl.BlockSpec((b, d), lambda l, j: (0, 0)),
                  pl.BlockSpec((1, d, d), lambda l, j: (l, 0, j)),
                  pl.BlockSpec((1, 1, d), lambda l, j: (l, 0, j))],
        out_specs=pl.BlockSpec((1, b, d), lambda l, j: (l, 0, j)),
        compiler_params=_params("parallel", "parallel"),
        name="adaln_modulation",
    )(c, w_mod, b_mod.reshape(depth, 1, d3))


def _rope_table_kernel(pos_ref, inv_ref, sgn_ref, cos_ref, sin_ref):
    ang = pos_ref[0].astype(F32) * inv_ref[...]
    cos_ref[0] = jnp.cos(ang)
    sin_ref[0] = jnp.sin(ang) * sgn_ref[...]


def _rope_tables(positions):
    b, s = positions.shape
    half = HEAD_DIM // 2
    inv = ROPE_THETA ** (-jnp.arange(half, dtype=F32) / half)
    reps = V7X_LANES // half
    inv_row = jnp.tile(inv, reps).reshape(1, V7X_LANES)
    sgn_row = jnp.tile(jnp.concatenate([-jnp.ones(half, F32), jnp.ones(half, F32)]), reps // 2).reshape(1, V7X_LANES)
    tab = jax.ShapeDtypeStruct((b, s, V7X_LANES), F32)
    row = pl.BlockSpec((1, V7X_LANES), lambda i, j: (0, 0))
    out = pl.BlockSpec((1, ROPE_TR, V7X_LANES), lambda i, j: (i, j, 0))
    return pl.pallas_call(
        _rope_table_kernel,
        out_shape=(tab, tab),
        grid=(b, s // ROPE_TR),
        in_specs=[pl.BlockSpec((1, ROPE_TR, 1), lambda i, j: (i, j, 0)), row, row],
        out_specs=(out, out),
        compiler_params=_params("parallel", "parallel"),
        name="rope_tables",
    )(positions.reshape(b, s, 1), inv_row, sgn_row)


def _rope(z, cos, sin):
    lane = lax.broadcasted_iota(jnp.int32, (1, V7X_LANES), 1)
    first_half = (lane % HEAD_DIM) < (HEAD_DIM // 2)
    outs = []
    for s in range(z.shape[1] // V7X_LANES):
        zs = z[:, s * V7X_LANES:(s + 1) * V7X_LANES]
        swapped = jnp.where(first_half, pltpu.roll(zs, V7X_LANES - HEAD_DIM // 2, 1), pltpu.roll(zs, HEAD_DIM // 2, 1))
        outs.append(zs * cos + swapped * sin)
    return jnp.concatenate(outs, axis=-1)


def _inproj_kernel(x_ref, mod_ref, w_ref, wvt_ref, cos_ref, sin_ref, wr_ref, br_ref,
                   aqk_ref, avt_ref, cq_ref, ck_ref, cv_ref, zg_ref, zb_ref, zd_ref):
    d = x_ref.shape[2]
    x = x_ref[0]
    mu = jnp.mean(x, axis=-1, keepdims=True)
    xc = x - mu
    var = jnp.mean(xc * xc, axis=-1, keepdims=True)
    xn = xc * lax.rsqrt(var + LN_EPS)
    shift = mod_ref[0, :, 0:d]
    scale = mod_ref[0, :, d:2 * d]
    u = (xn * (1.0 + scale) + shift).astype(BF16)

    def proj(off, width):
        return jnp.dot(u, w_ref[:, off:off + width], preferred_element_type=F32)

    cos = cos_ref[0]
    sin = sin_ref[0]
    q_scale = HEAD_DIM ** -0.5
    aqk_ref[0, :, 0:A_W] = (_rope(proj(OFF_AQ, A_W), cos, sin) * (q_scale * LOG2E)).astype(BF16)
    aqk_ref[0, :, A_W:2 * A_W] = _rope(proj(OFF_AK, A_W), cos, sin).astype(BF16)
    vt = _dot_nt(wvt_ref[...], u).astype(BF16)
    ones = jnp.ones((MOBA_VROWS - HEAD_DIM, vt.shape[1]), BF16)
    for h in range(MOBA_HEADS):
        avt_ref[0, h * MOBA_VROWS:h * MOBA_VROWS + HEAD_DIM, :] = vt[h * HEAD_DIM:(h + 1) * HEAD_DIM, :]
        avt_ref[0, h * MOBA_VROWS + HEAD_DIM:(h + 1) * MOBA_VROWS, :] = ones
    for ref, z in ((cq_ref, _rope(proj(OFF_CQ, C_W), cos, sin) * (q_scale * LOG2E)),
                   (ck_ref, _rope(proj(OFF_CK, C_W), cos, sin)),
                   (cv_ref, proj(OFF_CV, C_W))):
        for half in range(C_W // V7X_LANES):
            ref[0, half] = z[:, half * V7X_LANES:(half + 1) * V7X_LANES]
    for j, off in enumerate((OFF_AG, OFF_BG, OFF_CG, OFF_DG)):
        g = proj(off, A_W)
        zg_ref[0, :, j * A_W:(j + 1) * A_W] = (g * _sigmoid(g)).astype(BF16)
    zb_ref[0] = proj(OFF_BX, B_W)
    zd_ref[0, :, 0:D_KW] = proj(OFF_DQ, D_KW) * (GLA_DK ** -0.5)
    zd_ref[0, :, D_KW:2 * D_KW] = proj(OFF_DK, D_KW)
    zd_ref[0, :, 2 * D_KW:2 * D_KW + D_W] = proj(OFF_DV, D_W)
    dr = proj(OFF_DR, GLA_LOWRANK)
    la = jnp.dot(dr, wr_ref[...], preferred_element_type=F32, precision=HIGHEST) + br_ref[...]
    zd_ref[0, :, 2 * D_KW + D_W:] = _log_sigmoid(la) * (1.0 / GLA_TAU)


def _in_projection(x, mod_l, w_in_bf, wvt_bf, cos_t, sin_t, gla_wr, gla_br):
    b, s, d = x.shape
    tm = IN_TM
    row3 = lambda w: pl.BlockSpec((1, tm, w), lambda i, j: (i, j, 0))
    const2 = lambda shp: pl.BlockSpec(shp, lambda i, j: (0, 0))
    halves = pl.BlockSpec((1, C_W // V7X_LANES, tm, V7X_LANES), lambda i, j: (i, 0, j, 0))
    out_shape = (
        jax.ShapeDtypeStruct((b, s, 2 * A_W), BF16),
        jax.ShapeDtypeStruct((b, MOBA_HEADS * MOBA_VROWS, s), BF16),
        jax.ShapeDtypeStruct((b, C_W // V7X_LANES, s, V7X_LANES), F32),
        jax.ShapeDtypeStruct((b, C_W // V7X_LANES, s, V7X_LANES), F32),
        jax.ShapeDtypeStruct((b, C_W // V7X_LANES, s, V7X_LANES), F32),
        jax.ShapeDtypeStruct((b, s, 4 * A_W), BF16),
        jax.ShapeDtypeStruct((b, s, B_W), F32),
        jax.ShapeDtypeStruct((b, s, 2 * D_KW + D_W + D_KW), F32),
    )
    out_specs = (row3(2 * A_W), pl.BlockSpec((1, MOBA_HEADS * MOBA_VROWS, tm), lambda i, j: (i, 0, j)),
                 halves, halves, halves, row3(4 * A_W), row3(B_W), row3(2 * D_KW + D_W + D_KW))
    return pl.pallas_call(
        _inproj_kernel,
        out_shape=out_shape,
        grid=(b, s // tm),
        in_specs=[row3(d),
                  pl.BlockSpec((1, 1, mod_l.shape[-1]), lambda i, j: (i, 0, 0)),
                  const2(w_in_bf.shape), const2(wvt_bf.shape),
                  row3(V7X_LANES), row3(V7X_LANES),
                  const2(gla_wr.shape), const2((1, D_KW))],
        out_specs=out_specs,
        compiler_params=_params("parallel", "parallel"),
        name="in_projection",
    )(x, mod_l.reshape(b, 1, -1), w_in_bf, wvt_bf, cos_t, sin_t, gla_wr, gla_br.reshape(1, D_KW))


def _moba_kernel(q_ref, k_ref, vt_ref, o_ref, kmean_ref, qm_ref, unsel_ref, s_ref, p_ref, alpha_ref, acc_ref):
    blk = MOBA_BLOCK
    nb = k_ref.shape[1] // blk
    n = pl.program_id(1)

    @pl.when(n == 0)
    def _():
        for j in range(nb):
            kj = k_ref[0, j * blk:(j + 1) * blk, :].astype(F32)
            kmean_ref[j:j + 1, :] = jnp.mean(kj, axis=0, keepdims=True)

    q = q_ref[0]
    lane_head = lax.broadcasted_iota(jnp.int32, (1, A_W), 1) // HEAD_DIM
    km = kmean_ref[...]
    km_hi = km.astype(BF16)
    km_lo = (km - km_hi.astype(F32)).astype(BF16)
    key_idx = lax.broadcasted_iota(jnp.int32, (blk, blk), 0)
    qry_idx = lax.broadcasted_iota(jnp.int32, (blk, blk), 1)
    causal = key_idx <= qry_idx

    vr = MOBA_VROWS
    vth = lambda h, j: vt_ref[0, h * vr:(h + 1) * vr, pl.ds(pl.multiple_of(j * blk, blk), blk)]
    kblk = lambda j: k_ref[0, pl.ds(pl.multiple_of(j * blk, blk), blk), :]

    for h in range(MOBA_HEADS):
        qm_ref[h * blk:(h + 1) * blk, :] = jnp.where(lane_head == h, q, jnp.zeros_like(q))
    scores = lambda j: _dot_nt(kblk(j), qm_ref[...])

    blk_id = lax.broadcasted_iota(jnp.int32, (nb, MOBA_HEADS * blk), 0)
    gate = _dot_nt(km_hi, qm_ref[...]) + _dot_nt(km_lo, qm_ref[...])
    gate = jnp.where(blk_id < n, gate, NEG)
    s_own = scores(n)
    s_ref[0] = scores(0)
    rank = jnp.zeros(gate.shape, jnp.int32)
    for i in range(nb):
        gi = gate[i:i + 1, :]
        tie = (blk_id > i).astype(jnp.int32)
        rank = rank + jnp.where(gi > gate, 1, jnp.where(gi == gate, tie, 0))
    unsel_ref[...] = jnp.where((rank < MOBA_TOPK) & (blk_id < n), 0.0, 1.0)

    def accumulate(slot, j_blk):
        for h in range(MOBA_HEADS):
            cols = slice(h * blk, (h + 1) * blk)
            acc_ref[h * vr:(h + 1) * vr, :] = (alpha_ref[slot, :, cols] * acc_ref[h * vr:(h + 1) * vr, :]
                                              + jnp.dot(vth(h, j_blk), p_ref[slot, :, cols],
                                                        preferred_element_type=F32))

    acc_ref[...] = jnp.zeros_like(acc_ref)
    alpha_ref[1] = jnp.ones(alpha_ref.shape[1:], F32)
    s = jnp.where(jnp.concatenate([causal] * MOBA_HEADS, axis=1), s_own, NEG)
    m_init = jnp.max(s, axis=0, keepdims=True)
    p_ref[1] = jnp.exp2(s - m_init).astype(BF16)

    def step(j, j_prev, slot, m):
        s = s_ref[slot]
        s_ref[1 - slot] = scores(jnp.minimum(j + 1, nb - 1))
        accumulate(1 - slot, j_prev)
        unsel = unsel_ref[pl.ds(j, 1), :] > 0.5
        m_new = jnp.maximum(m, jnp.where(unsel, NEG, jnp.max(s, axis=0, keepdims=True)))
        alpha_ref[slot] = jnp.exp2(m - m_new)
        p_ref[slot] = jnp.exp2(s - jnp.where(unsel, -NEG, m_new)).astype(BF16)
        return m_new

    def body(i, m):
        m = step(2 * i, jnp.where(i == 0, n, 2 * i - 1), 0, m)
        return step(2 * i + 1, 2 * i, 1, m)

    pairs = (n + 1) // 2
    lax.fori_loop(0, pairs, body, m_init)
    accumulate(1, jnp.where(pairs == 0, n, 2 * pairs - 1))
    outs = []
    for h in range(MOBA_HEADS):
        a = acc_ref[h * vr:(h + 1) * vr, :]
        outs.append(a[0:HEAD_DIM, :] / a[HEAD_DIM:HEAD_DIM + 1, :])
    o_ref[0] = jnp.concatenate(outs, axis=0).T


def _moba(aqk, avt):
    b, s, _ = aqk.shape
    blk = MOBA_BLOCK
    return pl.pallas_call(
        _moba_kernel,
        out_shape=jax.ShapeDtypeStruct((b, s, A_W), F32),
        grid=(b, s // blk),
        in_specs=[pl.BlockSpec((1, blk, A_W), lambda i, n: (i, n, 0)),
                  pl.BlockSpec((1, s, A_W), lambda i, n: (i, 0, 1)),
                  pl.BlockSpec((1, MOBA_HEADS * MOBA_VROWS, s), lambda i, n: (i, 0, 0))],
        out_specs=pl.BlockSpec((1, blk, A_W), lambda i, n: (i, n, 0)),
        scratch_shapes=[pltpu.VMEM((s // blk, A_W), F32),
                        pltpu.VMEM((MOBA_HEADS * blk, A_W), BF16),
                        pltpu.VMEM((s // blk, MOBA_HEADS * blk), F32),
                        pltpu.VMEM((2, blk, MOBA_HEADS * blk), F32),
                        pltpu.VMEM((2, blk, MOBA_HEADS * blk), BF16),
                        pltpu.VMEM((2, 1, MOBA_HEADS * blk), F32),
                        pltpu.VMEM((MOBA_HEADS * MOBA_VROWS, blk), F32)],
        compiler_params=_params("arbitrary", "arbitrary"),
        name="moba_attention",
    )(aqk, aqk, avt)


def _lru_kernel(x_ref, cw_ref, cb_ref, wg_ref, bg_ref, lam_ref, o_ref, xs_ref, h_ref):
    tl = x_ref.shape[1]
    halo = V7X_SUBLANES
    i = pl.program_id(1)

    @pl.when(i == 0)
    def _():
        xs_ref[0:halo, :] = jnp.zeros((halo, B_W), F32)
        h_ref[...] = jnp.zeros_like(h_ref)

    x = x_ref[0]
    xs_ref[halo:, :] = x
    y = cb_ref[...] + cw_ref[CONV_WIDTH - 1:CONV_WIDTH, :] * x
    for k in range(CONV_WIDTH - 1):
        y = y + cw_ref[k:k + 1, :] * xs_ref[pl.ds(halo - (CONV_WIDTH - 1) + k, tl), :]
    xs_ref[0:halo, :] = x[tl - halo:, :]

    gates = jnp.dot(y.astype(BF16), wg_ref[...], preferred_element_type=F32) + bg_ref[...]
    r = _sigmoid(gates[:, 0:B_W])
    ig = _sigmoid(gates[:, B_W:])
    lam = lam_ref[...]
    softplus_neg_lam = jnp.maximum(-lam, 0.0) + jnp.log1p(jnp.exp(-jnp.abs(lam)))
    log_a = (-LRU_C) * r * softplus_neg_lam
    a = jnp.exp(log_a)
    u = jnp.sqrt(-jnp.tanh(log_a) * (a * a + 1.0)) * (ig * y)

    row = lax.broadcasted_iota(jnp.int32, (tl, 1), 0)
    step = 1
    while step < tl:
        keep = row >= step
        a_prev = jnp.where(keep, pltpu.roll(a, step, 0), 1.0)
        u_prev = jnp.where(keep, pltpu.roll(u, step, 0), 0.0)
        u = u + a * u_prev
        a = a * a_prev
        step *= 2
    h = u + a * h_ref[...]
    h_ref[...] = h[tl - 1:tl, :]
    o_ref[0] = h


def _rg_lru(zb, conv_w, conv_b, w_gates, b_gates, lam):
    b, s, w = zb.shape
    tl = LRU_TL
    const = lambda shp: pl.BlockSpec(shp, lambda i, j: (0, 0))
    return pl.pallas_call(
        _lru_kernel,
        out_shape=jax.ShapeDtypeStruct((b, s, w), F32),
        grid=(b, s // tl),
        in_specs=[pl.BlockSpec((1, tl, w), lambda i, j: (i, j, 0)),
                  const(conv_w.shape), const((1, w)), const(w_gates.shape), const((1, 2 * w)), const((1, w))],
        out_specs=pl.BlockSpec((1, tl, w), lambda i, j: (i, j, 0)),
        scratch_shapes=[pltpu.VMEM((tl + V7X_SUBLANES, w), F32), pltpu.VMEM((1, w), F32)],
        compiler_params=_params("arbitrary", "arbitrary"),
        name="rg_lru",
    )(zb, conv_w, conv_b.reshape(1, w), w_gates, b_gates.reshape(1, 2 * w), lam.reshape(1, w))


def _dilated_kernel(q_ref, k_ref, v_ref, o_ref, oacc_ref, lacc_ref):
    qb = DIL_QBLOCK
    tile = o_ref.shape[1]
    t0 = pl.program_id(1) * tile
    halves = range(C_W // V7X_LANES)

    def load(ref, rows):
        return jnp.concatenate([ref.at[half][rows, :] for half in halves], axis=-1)

    def store(ref, rows, value):
        for half in halves:
            ref.at[half][rows, :] = value[:, half * V7X_LANES:(half + 1) * V7X_LANES]

    lane_head = lax.broadcasted_iota(jnp.int32, (1, C_W), 1) // HEAD_DIM
    qi = lax.broadcasted_iota(jnp.int32, (qb, 2 * qb), 0)
    kj = lax.broadcasted_iota(jnp.int32, (qb, 2 * qb), 1)

    for ci, (window, d) in enumerate(DIL_CONFIGS):
        n_steps = window // d
        per_class = tile // (qb * d)
        first, last = ci == 0, ci == len(DIL_CONFIGS) - 1

        def units(i, carry, d=d, n_steps=n_steps, per_class=per_class, first=first, last=last):
            group = []
            for k in range(DIL_UNROLL):
                u = i * DIL_UNROLL + k
                r = u // per_class
                base = t0 // d + (u % per_class) * qb
                start = jnp.maximum(base - qb, 0)
                if d == 1:
                    tokens = lambda m0, count, r=r: pl.ds(pl.multiple_of(m0, qb), count)
                else:
                    tokens = lambda m0, count, r=r: pl.ds(m0 * d + r, count, stride=d)
                q = load(q_ref, tokens(base, qb))
                kk = load(k_ref, tokens(start, 2 * qb)).astype(BF16)
                vv = load(v_ref, tokens(start, 2 * qb)).astype(BF16)
                dist = (base - start) + qi - kj
                valid = (dist >= 0) & (dist <= n_steps)
                q_stack = jnp.concatenate([jnp.where(lane_head == h, q, 0.0) for h in range(DIL_HEADS)], axis=0)
                s = _dot_nt(q_stack.astype(BF16), kk)
                group.append((s, valid, vv, tokens(base - t0 // d, qb)))
            soft = []
            for s, valid, vv, local in group:
                s = jnp.where(jnp.concatenate([valid] * DIL_HEADS, axis=0), s, NEG)
                m = jnp.max(s, axis=-1, keepdims=True)
                e = jnp.exp2(s - m)
                soft.append((e.astype(BF16), m, jnp.sum(e, axis=-1, keepdims=True)))
            pvs = [jnp.dot(e, vv, preferred_element_type=F32)
                   for (e, _, _), (_, _, vv, _) in zip(soft, group)]
            for pv, (_, m, l), (_, _, _, local) in zip(pvs, soft, group):
                o_all = pv / l
                lse_all = m + jnp.log2(l)
                out = o_all[0:qb, :]
                lse = jnp.broadcast_to(lse_all[0:qb, :], (qb, C_W))
                for h in range(1, DIL_HEADS):
                    out = jnp.where(lane_head == h, o_all[h * qb:(h + 1) * qb, :], out)
                    lse = jnp.where(lane_head == h, lse_all[h * qb:(h + 1) * qb, :], lse)
                if not first:
                    o_prev = load(oacc_ref, local)
                    l_prev = load(lacc_ref, local)
                    top = jnp.maximum(l_prev, lse)
                    w_prev = jnp.exp2(l_prev - top)
                    w_cur = jnp.exp2(lse - top)
                    out = (w_prev * o_prev + w_cur * out) / (w_prev + w_cur)
                    lse = top + jnp.log2(w_prev + w_cur)
                if last:
                    store(o_ref, local, out)
                else:
                    store(oacc_ref, local, out)
                    store(lacc_ref, local, lse)
            return carry

        lax.fori_loop(0, tile // (qb * DIL_UNROLL), units, 0)


def _dilated(cq, ck, cv):
    b, nh, s, w = cq.shape
    tile = DIL_TILE
    whole = pl.BlockSpec((None, nh, s, w), lambda i, j: (i, 0, 0, 0))
    return pl.pallas_call(
        _dilated_kernel,
        out_shape=jax.ShapeDtypeStruct((b, nh, s, w), F32),
        grid=(b, s // tile),
        in_specs=[whole, whole, whole],
        out_specs=pl.BlockSpec((None, nh, tile, w), lambda i, j: (i, 0, j, 0)),
        scratch_shapes=[pltpu.VMEM((nh, tile, w), F32), pltpu.VMEM((nh, tile, w), F32)],
        compiler_params=_params("parallel", "arbitrary"),
        name="dilated_attention",
    )(cq, ck, cv)


def _gla_kernel(z_ref, gn_ref, o_ref, state_ref, b_ref, q_ref, k_ref, v_ref, t_ref, y_ref):
    c = GLA_CHUNK
    tg = z_ref.shape[1]
    nch = tg // c
    i = pl.program_id(1)

    @pl.when(i == 0)
    def _():
        state_ref[...] = jnp.zeros_like(state_ref)

    qs = z_ref[0, :, 0:D_KW]
    ks = z_ref[0, :, D_KW:2 * D_KW]
    vs = z_ref[0, :, 2 * D_KW:2 * D_KW + D_W]
    la = z_ref[0, :, 2 * D_KW + D_W:]

    ri = lax.broadcasted_iota(jnp.int32, (tg, tg), 0)
    ci = lax.broadcasted_iota(jnp.int32, (tg, tg), 1)
    chunk_causal = (ri // c == ci // c) & (ci <= ri)
    tri = jnp.where(chunk_causal, 1.0, 0.0).astype(BF16)
    la_hi = la.astype(BF16)
    la_mid = (la - la_hi.astype(F32)).astype(BF16)
    la_lo = (la - la_hi.astype(F32) - la_mid.astype(F32)).astype(BF16)
    b = (jnp.dot(tri, la_hi, preferred_element_type=F32) + jnp.dot(tri, la_mid, preferred_element_type=F32)
         + jnp.dot(tri, la_lo, preferred_element_type=F32))
    b_last = jnp.broadcast_to(b.reshape(nch, c, D_KW)[:, c - 1:c, :], (nch, c, D_KW)).reshape(tg, D_KW)
    q_in = qs * jnp.exp(b)
    k_dec = ks * jnp.exp(b_last - b)
    v_bf = vs.astype(BF16)

    safe = jnp.min(la) * c > -GLA_SAFE_DECAY

    @pl.when(safe)
    def _():
        k_grow = (ks * jnp.exp(-b)).astype(BF16)
        dk_head = lax.broadcasted_iota(jnp.int32, (1, D_KW), 1) // GLA_DK
        q_stack = jnp.concatenate([jnp.where(dk_head == h, q_in, 0.0) for h in range(GLA_HEADS)], axis=0)
        a = _dot_nt(q_stack.astype(BF16), k_grow)
        a = jnp.where(jnp.concatenate([chunk_causal] * GLA_HEADS, axis=0), a, 0.0)
        av = jnp.dot(a.astype(BF16), v_bf, preferred_element_type=F32)
        dv_head = lax.broadcasted_iota(jnp.int32, (1, D_W), 1) // GLA_DV
        y = av[0:tg, :]
        for h in range(1, GLA_HEADS):
            y = jnp.where(dv_head == h, av[h * tg:(h + 1) * tg, :], y)
        y_ref[...] = y

    @pl.when(jnp.logical_not(safe))
    def _():
        b_ref[...] = b
        q_ref[...] = qs
        k_ref[...] = ks
        v_ref[...] = vs
        head_sum = (lax.broadcasted_iota(jnp.int32, (D_KW, D_W), 0) // GLA_DK
                    == lax.broadcasted_iota(jnp.int32, (D_KW, D_W), 1) // GLA_DV).astype(BF16)
        jrow = lax.broadcasted_iota(jnp.int32, (c, D_KW), 0)

        def chunk(ch, carry):
            r0 = pl.multiple_of(ch * c, c)
            bch = b_ref[pl.ds(r0, c), :]
            kch = k_ref[pl.ds(r0, c), :]

            def row_terms(ii, carry2):
                bi = b_ref[pl.ds(r0 + ii, 1), :]
                qi = q_ref[pl.ds(r0 + ii, 1), :]
                t = jnp.exp(jnp.minimum(bi - bch, 0.0)) * kch * qi
                t_ref[pl.ds(pl.multiple_of(ii * c, c), c), :] = jnp.where(jrow <= ii, t, 0.0)
                return carry2

            lax.fori_loop(0, c, row_terms, 0)
            a_b = jnp.dot(t_ref[...].astype(BF16), head_sum, preferred_element_type=F32)
            y_ref[pl.ds(r0, c), :] = jnp.sum(a_b.reshape(c, c, D_W) * v_ref[pl.ds(r0, c), :][None, :, :], axis=1)
            return carry

        lax.fori_loop(0, nch, chunk, 0)

    same_head_t = (lax.broadcasted_iota(jnp.int32, (D_W, D_KW), 0) // GLA_DV
                   == lax.broadcasted_iota(jnp.int32, (D_W, D_KW), 1) // GLA_DK)
    q_in_bf = q_in.astype(BF16)
    k_dec_bf = k_dec.astype(BF16)
    st = state_ref[...]
    inter = []
    for ch in range(nch):
        r0 = ch * c
        inter.append(_dot_nt(q_in_bf[r0:r0 + c, :], st.astype(BF16)))
        kvt = lax.dot_general(v_bf[r0:r0 + c, :], k_dec_bf[r0:r0 + c, :], (((0,), (0,)), ((), ())),
                              preferred_element_type=F32)
        st = st * jnp.exp(b_last[r0:r0 + 1, :]) + jnp.where(same_head_t, kvt, 0.0)
    state_ref[...] = st
    y = y_ref[...] + jnp.concatenate(inter, axis=0)

    e_head = jnp.where(lax.broadcasted_iota(jnp.int32, (D_W, D_W), 0) // GLA_DV
                       == lax.broadcasted_iota(jnp.int32, (D_W, D_W), 1) // GLA_DV, 1.0 / GLA_DV, 0.0).astype(BF16)
    yy = y * y
    yy_hi = yy.astype(BF16)
    yy_lo = (yy - yy_hi.astype(F32)).astype(BF16)
    ms = (jnp.dot(yy_hi, e_head, preferred_element_type=F32) + jnp.dot(yy_lo, e_head, preferred_element_type=F32))
    o_ref[0] = y * lax.rsqrt(ms + LN_EPS) * gn_ref[...]


def _gla(zd, gla_gn):
    b, s, w = zd.shape
    tg = GLA_TG
    gn_row = jnp.tile(gla_gn.astype(F32), GLA_HEADS).reshape(1, D_W)
    return pl.pallas_call(
        _gla_kernel,
        out_shape=jax.ShapeDtypeStruct((b, s, D_W), F32),
        grid=(b, s // tg),
        in_specs=[pl.BlockSpec((1, tg, w), lambda i, j: (i, j, 0)),
                  pl.BlockSpec((1, D_W), lambda i, j: (0, 0))],
        out_specs=pl.BlockSpec((1, tg, D_W), lambda i, j: (i, j, 0)),
        scratch_shapes=[pltpu.VMEM((D_W, D_KW), F32),
                        pltpu.VMEM((tg, D_KW), F32),
                        pltpu.VMEM((tg, D_KW), F32),
                        pltpu.VMEM((tg, D_KW), F32),
                        pltpu.VMEM((tg, D_W), F32),
                        pltpu.VMEM((GLA_CHUNK * GLA_CHUNK, D_KW), F32),
                        pltpu.VMEM((tg, D_W), F32)],
        compiler_params=_params("arbitrary", "arbitrary"),
        name="gla",
    )(zd, gn_row)


def _outproj_kernel(ya_ref, yb_ref, yc_ref, yd_ref, zg_ref, x_ref, mod_ref, w_ref, g_ref, b_ref, o_ref, *, alpha):
    d = x_ref.shape[2]
    y = jnp.zeros((x_ref.shape[1], d), F32)
    yc = jnp.concatenate([yc_ref[0, half] for half in range(yc_ref.shape[1])], axis=-1)
    for j, yj in enumerate((ya_ref[0], yb_ref[0], yc, yd_ref[0])):
        mix = (yj * zg_ref[0, :, j * A_W:(j + 1) * A_W].astype(F32)).astype(BF16)
        y = y + jnp.dot(mix, w_ref[j * A_W:(j + 1) * A_W, :], preferred_element_type=F32)
    gate = mod_ref[0, :, 2 * d:3 * d]
    r = alpha * x_ref[0] + (1.0 + gate) * y
    mu = jnp.mean(r, axis=-1, keepdims=True)
    rc = r - mu
    var = jnp.mean(rc * rc, axis=-1, keepdims=True)
    o_ref[0] = rc * lax.rsqrt(var + LN_EPS) * g_ref[...] + b_ref[...]


def _out_projection(ya, yb, yc, yd, zg, x, mod_l, w_out_bf, ln_g, ln_b, alpha):
    b, s, d = x.shape
    tm = OUT_TM
    row3 = lambda w: pl.BlockSpec((1, tm, w), lambda i, j: (i, j, 0))
    const2 = lambda shp: pl.BlockSpec(shp, lambda i, j: (0, 0))
    return pl.pallas_call(
        functools.partial(_outproj_kernel, alpha=alpha),
        out_shape=jax.ShapeDtypeStruct((b, s, d), F32),
        grid=(b, s // tm),
        in_specs=[row3(A_W), row3(B_W),
                  pl.BlockSpec((1, yc.shape[1], tm, yc.shape[3]), lambda i, j: (i, 0, j, 0)),
                  row3(D_W), row3(4 * A_W), row3(d),
                  pl.BlockSpec((1, 1, mod_l.shape[-1]), lambda i, j: (i, 0, 0)),
                  const2(w_out_bf.shape), const2((1, d)), const2((1, d))],
        out_specs=row3(d),
        compiler_params=_params("parallel", "parallel"),
        name="out_projection",
    )(ya, yb, yc, yd, zg, x, mod_l.reshape(b, 1, -1), w_out_bf, ln_g.reshape(1, d), ln_b.reshape(1, d))


def _block_diag(w):
    g, bd, _ = w.shape
    eye = jnp.eye(g, dtype=w.dtype)
    return (eye[:, None, :, None] * w[:, :, None, :]).reshape(g * bd, g * bd)


def kernel(x, c, positions, w_mod, b_mod, w_in, conv_w, conv_b, lru_wa, lru_ba, lru_wx, lru_bx, lru_lam,
           gla_wr, gla_br, gla_gn, w_out, ln_g, ln_b):
    depth = w_mod.shape[0]
    alpha = (2 * depth) ** 0.25
    mod = _modulation(c, w_mod, b_mod)
    cos_t, sin_t = _rope_tables(positions)
    for l in range(depth):
        w_in_bf = w_in[l].astype(BF16)
        wvt_bf = w_in[l][:, OFF_AV:OFF_AV + A_W].T.astype(BF16)
        aqk, avt, cq, ck, cv, zg, zb, zd = _in_projection(x, mod[l], w_in_bf, wvt_bf, cos_t, sin_t,
                                                           gla_wr[l], gla_br[l])
        ya = _moba(aqk, avt)
        w_gates = jnp.concatenate([_block_diag(lru_wa[l]), _block_diag(lru_wx[l])], axis=1).astype(BF16)
        b_gates = jnp.concatenate([lru_ba[l], lru_bx[l]])
        yb = _rg_lru(zb, conv_w[l], conv_b[l], w_gates, b_gates, lru_lam[l])
        yc = _dilated(cq, ck, cv)
        yd = _gla(zd, gla_gn[l])
        x = _out_projection(ya, yb, yc, yd, zg, x, mod[l], w_out[l].astype(BF16), ln_g[l], ln_b[l], alpha)
    return x
```

```python
import functools

import jax
import jax.numpy as jnp
from jax import lax
from jax.experimental import pallas as pl
from jax.experimental.pallas import tpu as pltpu

F32 = jnp.float32
BF16 = jnp.bfloat16
HIGHEST = lax.Precision.HIGHEST

HEAD_DIM = 64
MOBA_HEADS = 4
MOBA_BLOCK = 256
MOBA_TOPK = 3
LRU_WIDTH = 256
LRU_BLOCKS = 4
CONV_WIDTH = 4
LRU_C = 8.0
DIL_HEADS = 4
DIL_CONFIGS = ((128, 1), (512, 4), (2048, 16))
DIL_QBLOCK = 128
GLA_HEADS = 4
GLA_DK = 32
GLA_DV = 64
GLA_LOWRANK = 16
GLA_TAU = 16.0
GLA_CHUNK = 32
ROPE_THETA = 10000.0
LN_EPS = 1e-5
NEG = -1e30
LOG2E = 1.4426950408889634
MOBA_VROWS = HEAD_DIM + 16
GLA_SAFE_DECAY = 60.0

A_W = MOBA_HEADS * HEAD_DIM
B_W = LRU_WIDTH
C_W = DIL_HEADS * HEAD_DIM
D_KW = GLA_HEADS * GLA_DK
D_W = GLA_HEADS * GLA_DV
OFF_AQ, OFF_AK, OFF_AV, OFF_AG = 0, A_W, 2 * A_W, 3 * A_W
OFF_BX = 4 * A_W
OFF_BG = OFF_BX + B_W
OFF_CQ = OFF_BG + B_W
OFF_CK, OFF_CV, OFF_CG = OFF_CQ + C_W, OFF_CQ + 2 * C_W, OFF_CQ + 3 * C_W
OFF_DQ = OFF_CQ + 4 * C_W
OFF_DK = OFF_DQ + D_KW
OFF_DV = OFF_DK + D_KW
OFF_DG = OFF_DV + D_W
OFF_DR = OFF_DG + D_W
D_IN = OFF_DR + GLA_LOWRANK

V7X_LANES = 128
V7X_SUBLANES = 8
V7X_VMEM_LIMIT_BYTES = 48 * 1024 * 1024

IN_TM = 512
OUT_TM = 512
LRU_TL = 512
GLA_TG = 256
ROPE_TR = 512
DIL_UNROLL = 2
DIL_TILE = DIL_QBLOCK * max(d for _, d in DIL_CONFIGS)


def _params(*sem):
    return pltpu.CompilerParams(dimension_semantics=sem, vmem_limit_bytes=V7X_VMEM_LIMIT_BYTES)


def _sigmoid(x):
    return 1.0 / (1.0 + jnp.exp(-x))


def _log_sigmoid(x):
    return jnp.minimum(x, 0.0) - jnp.log1p(jnp.exp(-jnp.abs(x)))


def _dot_nt(a, b, **kw):
    return lax.dot_general(a, b, (((1,), (1,)), ((), ())), preferred_element_type=F32, **kw)


def _mod_kernel(c_ref, w_ref, b_ref, o_ref):
    o_ref[0] = jnp.dot(c_ref[...], w_ref[0], preferred_element_type=F32, precision=HIGHEST) + b_ref[0]


def _modulation(c, w_mod, b_mod):
    depth, d, d3 = w_mod.shape
    b = c.shape[0]
    nt = d3 // d
    return pl.pallas_call(
        _mod_kernel,
        out_shape=jax.ShapeDtypeStruct((depth, b, d3), F32),
        grid=(depth, nt),
        in_specs=[pl.BlockSpec((b, d), lambda l, j: (0, 0)),
                  pl.BlockSpec((1, d, d), lambda l, j: (l, 0, j)),
                  pl.BlockSpec((1, 1, d), lambda l, j: (l, 0, j))],
        out_specs=pl.BlockSpec((1, b, d), lambda l, j: (l, 0, j)),
        compiler_params=_params("parallel", "parallel"),
        name="adaln_modulation",
    )(c, w_mod, b_mod.reshape(depth, 1, d3))


def _rope_table_kernel(pos_ref, inv_ref, sgn_ref, cos_ref, sin_ref):
    ang = pos_ref[0].astype(F32) * inv_ref[...]
    cos_ref[0] = jnp.cos(ang)
    sin_ref[0] = jnp.sin(ang) * sgn_ref[...]


def _rope_tables(positions):
    b, s = positions.shape
    half = HEAD_DIM // 2
    inv = ROPE_THETA ** (-jnp.arange(half, dtype=F32) / half)
    reps = V7X_LANES // half
    inv_row = jnp.tile(inv, reps).reshape(1, V7X_LANES)
    sgn_row = jnp.tile(jnp.concatenate([-jnp.ones(half, F32), jnp.ones(half, F32)]), reps // 2).reshape(1, V7X_LANES)
    tab = jax.ShapeDtypeStruct((b, s, V7X_LANES), F32)
    row = pl.BlockSpec((1, V7X_LANES), lambda i, j: (0, 0))
    out = pl.BlockSpec((1, ROPE_TR, V7X_LANES), lambda i, j: (i, j, 0))
    return pl.pallas_call(
        _rope_table_kernel,
        out_shape=(tab, tab),
        grid=(b, s // ROPE_TR),
        in_specs=[pl.BlockSpec((1, ROPE_TR, 1), lambda i, j: (i, j, 0)), row, row],
        out_specs=(out, out),
        compiler_params=_params("parallel", "parallel"),
        name="rope_tables",
    )(positions.reshape(b, s, 1), inv_row, sgn_row)


def _rope(z, cos, sin):
    lane = lax.broadcasted_iota(jnp.int32, (1, V7X_LANES), 1)
    first_half = (lane % HEAD_DIM) < (HEAD_DIM // 2)
    outs = []
    for s in range(z.shape[1] // V7X_LANES):
        zs = z[:, s * V7X_LANES:(s + 1) * V7X_LANES]
        swapped = jnp.where(first_half, pltpu.roll(zs, V7X_LANES - HEAD_DIM // 2, 1), pltpu.roll(zs, HEAD_DIM // 2, 1))
        outs.append(zs * cos + swapped * sin)
    return jnp.concatenate(outs, axis=-1)


def _inproj_kernel(x_ref, mod_ref, w_ref, wvt_ref, cos_ref, sin_ref, wr_ref, br_ref,
                   aq_ref, ak_ref, avt_ref, cq_ref, ck_ref, cv_ref, zg_ref, zb_ref, zd_ref):
    d = x_ref.shape[2]
    x = x_ref[0]
    mu = jnp.mean(x, axis=-1, keepdims=True)
    xc = x - mu
    var = jnp.mean(xc * xc, axis=-1, keepdims=True)
    xn = xc * lax.rsqrt(var + LN_EPS)
    shift = mod_ref[0, :, 0:d]
    scale = mod_ref[0, :, d:2 * d]
    u = (xn * (1.0 + scale) + shift).astype(BF16)

    def proj(off, width):
        return jnp.dot(u, w_ref[:, off:off + width], preferred_element_type=F32)

    cos = cos_ref[0]
    sin = sin_ref[0]
    q_scale = HEAD_DIM ** -0.5
    aq_ref[0] = (_rope(proj(OFF_AQ, A_W), cos, sin) * (q_scale * LOG2E)).astype(BF16)
    ak_ref[0] = _rope(proj(OFF_AK, A_W), cos, sin).astype(BF16)
    vt = _dot_nt(wvt_ref[...], u).astype(BF16)
    ones = jnp.ones((MOBA_VROWS - HEAD_DIM, vt.shape[1]), BF16)
    for h in range(MOBA_HEADS):
        avt_ref[0, h * MOBA_VROWS:h * MOBA_VROWS + HEAD_DIM, :] = vt[h * HEAD_DIM:(h + 1) * HEAD_DIM, :]
        avt_ref[0, h * MOBA_VROWS + HEAD_DIM:(h + 1) * MOBA_VROWS, :] = ones
    for ref, z in ((cq_ref, _rope(proj(OFF_CQ, C_W), cos, sin) * (q_scale * LOG2E)),
                   (ck_ref, _rope(proj(OFF_CK, C_W), cos, sin)),
                   (cv_ref, proj(OFF_CV, C_W))):
        for half in range(C_W // V7X_LANES):
            ref[0, half] = z[:, half * V7X_LANES:(half + 1) * V7X_LANES]
    for j, off in enumerate((OFF_AG, OFF_BG, OFF_CG, OFF_DG)):
        g = proj(off, A_W)
        zg_ref[0, :, j * A_W:(j + 1) * A_W] = (g * _sigmoid(g)).astype(BF16)
    zb_ref[0] = proj(OFF_BX, B_W)
    zd_ref[0, :, 0:D_KW] = proj(OFF_DQ, D_KW) * (GLA_DK ** -0.5)
    zd_ref[0, :, D_KW:2 * D_KW] = proj(OFF_DK, D_KW)
    zd_ref[0, :, 2 * D_KW:2 * D_KW + D_W] = proj(OFF_DV, D_W)
    dr = proj(OFF_DR, GLA_LOWRANK)
    la = jnp.dot(dr, wr_ref[...], preferred_element_type=F32, precision=HIGHEST) + br_ref[...]
    zd_ref[0, :, 2 * D_KW + D_W:] = _log_sigmoid(la) * (1.0 / GLA_TAU)


def _in_projection(x, mod_l, w_in_bf, wvt_bf, cos_t, sin_t, gla_wr, gla_br):
    b, s, d = x.shape
    tm = IN_TM
    row3 = lambda w: pl.BlockSpec((1, tm, w), lambda i, j: (i, j, 0))
    const2 = lambda shp: pl.BlockSpec(shp, lambda i, j: (0, 0))
    halves = pl.BlockSpec((1, C_W // V7X_LANES, tm, V7X_LANES), lambda i, j: (i, 0, j, 0))
    out_shape = (
        jax.ShapeDtypeStruct((b, s, A_W), BF16),
        jax.ShapeDtypeStruct((b, s, A_W), BF16),
        jax.ShapeDtypeStruct((b, MOBA_HEADS * MOBA_VROWS, s), BF16),
        jax.ShapeDtypeStruct((b, C_W // V7X_LANES, s, V7X_LANES), F32),
        jax.ShapeDtypeStruct((b, C_W // V7X_LANES, s, V7X_LANES), F32),
        jax.ShapeDtypeStruct((b, C_W // V7X_LANES, s, V7X_LANES), F32),
        jax.ShapeDtypeStruct((b, s, 4 * A_W), BF16),
        jax.ShapeDtypeStruct((b, s, B_W), F32),
        jax.ShapeDtypeStruct((b, s, 2 * D_KW + D_W + D_KW), F32),
    )
    out_specs = (row3(A_W), row3(A_W), pl.BlockSpec((1, MOBA_HEADS * MOBA_VROWS, tm), lambda i, j: (i, 0, j)),
                 halves, halves, halves, row3(4 * A_W), row3(B_W), row3(2 * D_KW + D_W + D_KW))
    return pl.pallas_call(
        _inproj_kernel,
        out_shape=out_shape,
        grid=(b, s // tm),
        in_specs=[row3(d),
                  pl.BlockSpec((1, 1, mod_l.shape[-1]), lambda i, j: (i, 0, 0)),
                  const2(w_in_bf.shape), const2(wvt_bf.shape),
                  row3(V7X_LANES), row3(V7X_LANES),
                  const2(gla_wr.shape), const2((1, D_KW))],
        out_specs=out_specs,
        compiler_params=_params("parallel", "parallel"),
        name="in_projection",
    )(x, mod_l.reshape(b, 1, -1), w_in_bf, wvt_bf, cos_t, sin_t, gla_wr, gla_br.reshape(1, D_KW))


def _moba_kernel(q_ref, k_ref, vt_ref, o_ref, kmean_ref, qm_ref, unsel_ref, s_ref, p_ref, alpha_ref, acc_ref):
    blk = MOBA_BLOCK
    nb = k_ref.shape[1] // blk
    n = pl.program_id(1)

    @pl.when(n == 0)
    def _():
        for j in range(nb):
            kj = k_ref[0, j * blk:(j + 1) * blk, :].astype(F32)
            kmean_ref[j:j + 1, :] = jnp.mean(kj, axis=0, keepdims=True)

    q = q_ref[0]
    lane_head = lax.broadcasted_iota(jnp.int32, (1, A_W), 1) // HEAD_DIM
    km = kmean_ref[...]
    km_hi = km.astype(BF16)
    km_lo = (km - km_hi.astype(F32)).astype(BF16)
    key_idx = lax.broadcasted_iota(jnp.int32, (blk, blk), 0)
    qry_idx = lax.broadcasted_iota(jnp.int32, (blk, blk), 1)
    causal = key_idx <= qry_idx

    vr = MOBA_VROWS
    vth = lambda h, j: vt_ref[0, h * vr:(h + 1) * vr, pl.ds(pl.multiple_of(j * blk, blk), blk)]
    kblk = lambda j: k_ref[0, pl.ds(pl.multiple_of(j * blk, blk), blk), :]

    for h in range(MOBA_HEADS):
        qm_ref[h * blk:(h + 1) * blk, :] = jnp.where(lane_head == h, q, jnp.zeros_like(q))
    scores = lambda j: _dot_nt(kblk(j), qm_ref[...])

    blk_id = lax.broadcasted_iota(jnp.int32, (nb, MOBA_HEADS * blk), 0)
    own = _dot_nt(jnp.concatenate([kblk(n), km_hi, km_lo], axis=0), qm_ref[...])
    s_own = own[0:blk, :]
    gate = jnp.where(blk_id < n, own[blk:blk + nb, :] + own[blk + nb:blk + 2 * nb, :], NEG)
    s_ref[0] = scores(0)
    rank = jnp.zeros(gate.shape, jnp.int32)
    for i in range(nb):
        gi = gate[i:i + 1, :]
        tie = (blk_id > i).astype(jnp.int32)
        rank = rank + jnp.where(gi > gate, 1, jnp.where(gi == gate, tie, 0))
    unsel_ref[...] = jnp.where((rank < MOBA_TOPK) & (blk_id < n), 0.0, 1.0)

    def accumulate(slot, j_blk):
        for h in range(MOBA_HEADS):
            cols = slice(h * blk, (h + 1) * blk)
            acc_ref[h * vr:(h + 1) * vr, :] = (alpha_ref[slot, :, cols] * acc_ref[h * vr:(h + 1) * vr, :]
                                              + jnp.dot(vth(h, j_blk), p_ref[slot, :, cols],
                                                        preferred_element_type=F32))

    acc_ref[...] = jnp.zeros_like(acc_ref)
    alpha_ref[1] = jnp.ones(alpha_ref.shape[1:], F32)
    s = jnp.where(jnp.concatenate([causal] * MOBA_HEADS, axis=1), s_own, NEG)
    m_init = jnp.max(s, axis=0, keepdims=True)
    p_ref[1] = jnp.exp2(s - m_init).astype(BF16)

    def step(j, j_prev, slot, m):
        s = s_ref[slot]
        s_ref[1 - slot] = scores(jnp.minimum(j + 1, nb - 1))
        accumulate(1 - slot, j_prev)
        unsel = unsel_ref[pl.ds(j, 1), :] > 0.5
        m_new = jnp.maximum(m, jnp.where(unsel, NEG, jnp.max(s, axis=0, keepdims=True)))
        alpha_ref[slot] = jnp.exp2(m - m_new)
        p_ref[slot] = jnp.exp2(s - jnp.where(unsel, -NEG, m_new)).astype(BF16)
        return m_new

    def body(i, m):
        m = step(2 * i, jnp.where(i == 0, n, 2 * i - 1), 0, m)
        return step(2 * i + 1, 2 * i, 1, m)

    pairs = (n + 1) // 2
    lax.fori_loop(0, pairs, body, m_init)
    accumulate(1, jnp.where(pairs == 0, n, 2 * pairs - 1))
    outs = []
    for h in range(MOBA_HEADS):
        a = acc_ref[h * vr:(h + 1) * vr, :]
        outs.append(a[0:HEAD_DIM, :] / a[HEAD_DIM:HEAD_DIM + 1, :])
    o_ref[0] = jnp.concatenate(outs, axis=0).T.astype(o_ref.dtype)


def _moba(aq, ak, avt):
    b, s, _ = aq.shape
    blk = MOBA_BLOCK
    return pl.pallas_call(
        _moba_kernel,
        out_shape=jax.ShapeDtypeStruct((b, s, A_W), BF16),
        grid=(b, s // blk),
        in_specs=[pl.BlockSpec((1, blk, A_W), lambda i, n: (i, n, 0)),
                  pl.BlockSpec((1, s, A_W), lambda i, n: (i, 0, 0)),
                  pl.BlockSpec((1, MOBA_HEADS * MOBA_VROWS, s), lambda i, n: (i, 0, 0))],
        out_specs=pl.BlockSpec((1, blk, A_W), lambda i, n: (i, n, 0)),
        scratch_shapes=[pltpu.VMEM((s // blk, A_W), F32),
                        pltpu.VMEM((MOBA_HEADS * blk, A_W), BF16),
                        pltpu.VMEM((s // blk, MOBA_HEADS * blk), F32),
                        pltpu.VMEM((2, blk, MOBA_HEADS * blk), F32),
                        pltpu.VMEM((2, blk, MOBA_HEADS * blk), BF16),
                        pltpu.VMEM((2, 1, MOBA_HEADS * blk), F32),
                        pltpu.VMEM((MOBA_HEADS * MOBA_VROWS, blk), F32)],
        compiler_params=_params("arbitrary", "arbitrary"),
        name="moba_attention",
    )(aq, ak, avt)


def _lru_kernel(x_ref, cw_ref, cb_ref, wg_ref, bg_ref, lam_ref, o_ref, xs_ref, h_ref):
    tl = x_ref.shape[1]
    halo = V7X_SUBLANES
    i = pl.program_id(1)

    @pl.when(i == 0)
    def _():
        xs_ref[0:halo, :] = jnp.zeros((halo, B_W), F32)
        h_ref[...] = jnp.zeros_like(h_ref)

    x = x_ref[0]
    xs_ref[halo:, :] = x
    y = cb_ref[...] + cw_ref[CONV_WIDTH - 1:CONV_WIDTH, :] * x
    for k in range(CONV_WIDTH - 1):
        y = y + cw_ref[k:k + 1, :] * xs_ref[pl.ds(halo - (CONV_WIDTH - 1) + k, tl), :]
    xs_ref[0:halo, :] = x[tl - halo:, :]

    gates = jnp.dot(y.astype(BF16), wg_ref[...], preferred_element_type=F32) + bg_ref[...]
    r = _sigmoid(gates[:, 0:B_W])
    ig = _sigmoid(gates[:, B_W:])
    lam = lam_ref[...]
    softplus_neg_lam = jnp.maximum(-lam, 0.0) + jnp.log1p(jnp.exp(-jnp.abs(lam)))
    log_a = (-LRU_C) * r * softplus_neg_lam
    a = jnp.exp(log_a)
    u = jnp.sqrt(-jnp.tanh(log_a) * (a * a + 1.0)) * (ig * y)

    row = lax.broadcasted_iota(jnp.int32, (tl, 1), 0)
    step = 1
    while step < tl:
        keep = row >= step
        a_prev = jnp.where(keep, pltpu.roll(a, step, 0), 1.0)
        u_prev = jnp.where(keep, pltpu.roll(u, step, 0), 0.0)
        u = u + a * u_prev
        a = a * a_prev
        step *= 2
    h = u + a * h_ref[...]
    h_ref[...] = h[tl - 1:tl, :]
    o_ref[0] = h.astype(o_ref.dtype)


def _rg_lru(zb, conv_w, conv_b, w_gates, b_gates, lam):
    b, s, w = zb.shape
    tl = LRU_TL
    const = lambda shp: pl.BlockSpec(shp, lambda i, j: (0, 0))
    return pl.pallas_call(
        _lru_kernel,
        out_shape=jax.ShapeDtypeStruct((b, s, w), BF16),
        grid=(b, s // tl),
        in_specs=[pl.BlockSpec((1, tl, w), lambda i, j: (i, j, 0)),
                  const(conv_w.shape), const((1, w)), const(w_gates.shape), const((1, 2 * w)), const((1, w))],
        out_specs=pl.BlockSpec((1, tl, w), lambda i, j: (i, j, 0)),
        scratch_shapes=[pltpu.VMEM((tl + V7X_SUBLANES, w), F32), pltpu.VMEM((1, w), F32)],
        compiler_params=_params("arbitrary", "arbitrary"),
        name="rg_lru",
    )(zb, conv_w, conv_b.reshape(1, w), w_gates, b_gates.reshape(1, 2 * w), lam.reshape(1, w))


def _dilated_kernel(q_ref, k_ref, v_ref, o_ref, oacc_ref, lacc_ref):
    qb = DIL_QBLOCK
    tile = o_ref.shape[1]
    t0 = pl.program_id(1) * tile
    halves = range(C_W // V7X_LANES)

    def load(ref, rows):
        return jnp.concatenate([ref.at[half][rows, :] for half in halves], axis=-1)

    def store(ref, rows, value):
        for half in halves:
            ref.at[half][rows, :] = value[:, half * V7X_LANES:(half + 1) * V7X_LANES]

    lane_head = lax.broadcasted_iota(jnp.int32, (1, C_W), 1) // HEAD_DIM
    qi = lax.broadcasted_iota(jnp.int32, (qb, 2 * qb), 0)
    kj = lax.broadcasted_iota(jnp.int32, (qb, 2 * qb), 1)

    for ci, (window, d) in enumerate(DIL_CONFIGS):
        n_steps = window // d
        per_class = tile // (qb * d)
        first, last = ci == 0, ci == len(DIL_CONFIGS) - 1

        def units(i, carry, d=d, n_steps=n_steps, per_class=per_class, first=first, last=last):
            group = []
            for k in range(DIL_UNROLL):
                u = i * DIL_UNROLL + k
                r = u // per_class
                base = t0 // d + (u % per_class) * qb
                start = jnp.maximum(base - qb, 0)
                if d == 1:
                    tokens = lambda m0, count, r=r: pl.ds(pl.multiple_of(m0, qb), count)
                else:
                    tokens = lambda m0, count, r=r: pl.ds(m0 * d + r, count, stride=d)
                q = load(q_ref, tokens(base, qb))
                kk = load(k_ref, tokens(start, 2 * qb)).astype(BF16)
                vv = load(v_ref, tokens(start, 2 * qb)).astype(BF16)
                dist = (base - start) + qi - kj
                valid = (dist >= 0) & (dist <= n_steps)
                q_stack = jnp.concatenate([jnp.where(lane_head == h, q, 0.0) for h in range(DIL_HEADS)], axis=0)
                s = _dot_nt(q_stack.astype(BF16), kk)
                group.append((s, valid, vv, tokens(base - t0 // d, qb)))
            soft = []
            for s, valid, vv, local in group:
                s = jnp.where(jnp.concatenate([valid] * DIL_HEADS, axis=0), s, NEG)
                m = jnp.max(s, axis=-1, keepdims=True)
                e = jnp.exp2(s - m)
                soft.append((e.astype(BF16), m, jnp.sum(e, axis=-1, keepdims=True)))
            pvs = [jnp.dot(e, vv, preferred_element_type=F32)
                   for (e, _, _), (_, _, vv, _) in zip(soft, group)]
            for pv, (_, m, l), (_, _, _, local) in zip(pvs, soft, group):
                o_all = pv / l
                lse_all = m + jnp.log2(l)
                out = o_all[0:qb, :]
                lse = jnp.broadcast_to(lse_all[0:qb, :], (qb, C_W))
                for h in range(1, DIL_HEADS):
                    out = jnp.where(lane_head == h, o_all[h * qb:(h + 1) * qb, :], out)
                    lse = jnp.where(lane_head == h, lse_all[h * qb:(h + 1) * qb, :], lse)
                if not first:
                    o_prev = load(oacc_ref, local)
                    l_prev = load(lacc_ref, local)
                    top = jnp.maximum(l_prev, lse)
                    w_prev = jnp.exp2(l_prev - top)
                    w_cur = jnp.exp2(lse - top)
                    out = (w_prev * o_prev + w_cur * out) / (w_prev + w_cur)
                    lse = top + jnp.log2(w_prev + w_cur)
                if last:
                    store(o_ref, local, out)
                else:
                    store(oacc_ref, local, out)
                    store(lacc_ref, local, lse)
            return carry

        lax.fori_loop(0, tile // (qb * DIL_UNROLL), units, 0)


def _dilated(cq, ck, cv):
    b, nh, s, w = cq.shape
    tile = DIL_TILE
    whole = pl.BlockSpec((None, nh, s, w), lambda i, j: (i, 0, 0, 0))
    return pl.pallas_call(
        _dilated_kernel,
        out_shape=jax.ShapeDtypeStruct((b, nh, s, w), F32),
        grid=(b, s // tile),
        in_specs=[whole, whole, whole],
        out_specs=pl.BlockSpec((None, nh, tile, w), lambda i, j: (i, 0, j, 0)),
        scratch_shapes=[pltpu.VMEM((nh, tile, w), F32), pltpu.VMEM((nh, tile, w), F32)],
        compiler_params=_params("parallel", "arbitrary"),
        name="dilated_attention",
    )(cq, ck, cv)


def _gla_kernel(z_ref, gn_ref, o_ref, state_ref, b_ref, q_ref, k_ref, v_ref, t_ref, y_ref):
    c = GLA_CHUNK
    tg = z_ref.shape[1]
    nch = tg // c
    i = pl.program_id(1)

    @pl.when(i == 0)
    def _():
        state_ref[...] = jnp.zeros_like(state_ref)

    qs = z_ref[0, :, 0:D_KW]
    ks = z_ref[0, :, D_KW:2 * D_KW]
    vs = z_ref[0, :, 2 * D_KW:2 * D_KW + D_W]
    la = z_ref[0, :, 2 * D_KW + D_W:]

    ri = lax.broadcasted_iota(jnp.int32, (tg, tg), 0)
    ci = lax.broadcasted_iota(jnp.int32, (tg, tg), 1)
    chunk_causal = (ri // c == ci // c) & (ci <= ri)
    tri = jnp.where(chunk_causal, 1.0, 0.0).astype(BF16)
    la_hi = la.astype(BF16)
    la_mid = (la - la_hi.astype(F32)).astype(BF16)
    la_lo = (la - la_hi.astype(F32) - la_mid.astype(F32)).astype(BF16)
    b = (jnp.dot(tri, la_hi, preferred_element_type=F32) + jnp.dot(tri, la_mid, preferred_element_type=F32)
         + jnp.dot(tri, la_lo, preferred_element_type=F32))
    b_last = jnp.broadcast_to(b.reshape(nch, c, D_KW)[:, c - 1:c, :], (nch, c, D_KW)).reshape(tg, D_KW)
    q_in = qs * jnp.exp(b)
    k_dec = ks * jnp.exp(b_last - b)
    v_bf = vs.astype(BF16)

    safe = jnp.min(la) * c > -GLA_SAFE_DECAY

    @pl.when(safe)
    def _():
        k_grow = (ks * jnp.exp(-b)).astype(BF16)
        dk_head = lax.broadcasted_iota(jnp.int32, (1, D_KW), 1) // GLA_DK
        q_stack = jnp.concatenate([jnp.where(dk_head == h, q_in, 0.0) for h in range(GLA_HEADS)], axis=0)
        a = _dot_nt(q_stack.astype(BF16), k_grow)
        a = jnp.where(jnp.concatenate([chunk_causal] * GLA_HEADS, axis=0), a, 0.0)
        av = jnp.dot(a.astype(BF16), v_bf, preferred_element_type=F32)
        dv_head = lax.broadcasted_iota(jnp.int32, (1, D_W), 1) // GLA_DV
        y = av[0:tg, :]
        for h in range(1, GLA_HEADS):
            y = jnp.where(dv_head == h, av[h * tg:(h + 1) * tg, :], y)
        y_ref[...] = y

    @pl.when(jnp.logical_not(safe))
    def _():
        b_ref[...] = b
        q_ref[...] = qs
        k_ref[...] = ks
        v_ref[...] = vs
        head_sum = (lax.broadcasted_iota(jnp.int32, (D_KW, D_W), 0) // GLA_DK
                    == lax.broadcasted_iota(jnp.int32, (D_KW, D_W), 1) // GLA_DV).astype(BF16)
        jrow = lax.broadcasted_iota(jnp.int32, (c, D_KW), 0)

        def chunk(ch, carry):
            r0 = pl.multiple_of(ch * c, c)
            bch = b_ref[pl.ds(r0, c), :]
            kch = k_ref[pl.ds(r0, c), :]

            def row_terms(ii, carry2):
                bi = b_ref[pl.ds(r0 + ii, 1), :]
                qi = q_ref[pl.ds(r0 + ii, 1), :]
                t = jnp.exp(jnp.minimum(bi - bch, 0.0)) * kch * qi
                t_ref[pl.ds(pl.multiple_of(ii * c, c), c), :] = jnp.where(jrow <= ii, t, 0.0)
                return carry2

            lax.fori_loop(0, c, row_terms, 0)
            a_b = jnp.dot(t_ref[...].astype(BF16), head_sum, preferred_element_type=F32)
            y_ref[pl.ds(r0, c), :] = jnp.sum(a_b.reshape(c, c, D_W) * v_ref[pl.ds(r0, c), :][None, :, :], axis=1)
            return carry

        lax.fori_loop(0, nch, chunk, 0)

    same_head_t = (lax.broadcasted_iota(jnp.int32, (D_W, D_KW), 0) // GLA_DV
                   == lax.broadcasted_iota(jnp.int32, (D_W, D_KW), 1) // GLA_DK)
    q_in_bf = q_in.astype(BF16)
    k_dec_bf = k_dec.astype(BF16)
    st = state_ref[...]
    inter = []
    for ch in range(nch):
        r0 = ch * c
        inter.append(_dot_nt(q_in_bf[r0:r0 + c, :], st.astype(BF16)))
        kvt = lax.dot_general(v_bf[r0:r0 + c, :], k_dec_bf[r0:r0 + c, :], (((0,), (0,)), ((), ())),
                              preferred_element_type=F32)
        st = st * jnp.exp(b_last[r0:r0 + 1, :]) + jnp.where(same_head_t, kvt, 0.0)
    state_ref[...] = st
    y = y_ref[...] + jnp.concatenate(inter, axis=0)

    e_head = jnp.where(lax.broadcasted_iota(jnp.int32, (D_W, D_W), 0) // GLA_DV
                       == lax.broadcasted_iota(jnp.int32, (D_W, D_W), 1) // GLA_DV, 1.0 / GLA_DV, 0.0).astype(BF16)
    yy = y * y
    yy_hi = yy.astype(BF16)
    yy_lo = (yy - yy_hi.astype(F32)).astype(BF16)
    ms = (jnp.dot(yy_hi, e_head, preferred_element_type=F32) + jnp.dot(yy_lo, e_head, preferred_element_type=F32))
    o_ref[0] = (y * lax.rsqrt(ms + LN_EPS) * gn_ref[...]).astype(o_ref.dtype)


def _gla(zd, gla_gn):
    b, s, w = zd.shape
    tg = GLA_TG
    gn_row = jnp.tile(gla_gn.astype(F32), GLA_HEADS).reshape(1, D_W)
    return pl.pallas_call(
        _gla_kernel,
        out_shape=jax.ShapeDtypeStruct((b, s, D_W), BF16),
        grid=(b, s // tg),
        in_specs=[pl.BlockSpec((1, tg, w), lambda i, j: (i, j, 0)),
                  pl.BlockSpec((1, D_W), lambda i, j: (0, 0))],
        out_specs=pl.BlockSpec((1, tg, D_W), lambda i, j: (i, j, 0)),
        scratch_shapes=[pltpu.VMEM((D_W, D_KW), F32),
                        pltpu.VMEM((tg, D_KW), F32),
                        pltpu.VMEM((tg, D_KW), F32),
                        pltpu.VMEM((tg, D_KW), F32),
                        pltpu.VMEM((tg, D_W), F32),
                        pltpu.VMEM((GLA_CHUNK * GLA_CHUNK, D_KW), F32),
                        pltpu.VMEM((tg, D_W), F32)],
        compiler_params=_params("arbitrary", "arbitrary"),
        name="gla",
    )(zd, gn_row)


def _outproj_kernel(ya_ref, yb_ref, yc_ref, yd_ref, zg_ref, x_ref, mod_ref, w_ref, g_ref, b_ref, o_ref, *, alpha):
    d = x_ref.shape[2]
    y = jnp.zeros((x_ref.shape[1], d), F32)
    yc = jnp.concatenate([yc_ref[0, half] for half in range(yc_ref.shape[1])], axis=-1)
    for j, yj in enumerate((ya_ref[0], yb_ref[0], yc, yd_ref[0])):
        mix = (yj.astype(F32) * zg_ref[0, :, j * A_W:(j + 1) * A_W].astype(F32)).astype(BF16)
        y = y + jnp.dot(mix, w_ref[j * A_W:(j + 1) * A_W, :], preferred_element_type=F32)
    gate = mod_ref[0, :, 2 * d:3 * d]
    r = alpha * x_ref[0] + (1.0 + gate) * y
    mu = jnp.mean(r, axis=-1, keepdims=True)
    rc = r - mu
    var = jnp.mean(rc * rc, axis=-1, keepdims=True)
    o_ref[0] = rc * lax.rsqrt(var + LN_EPS) * g_ref[...] + b_ref[...]


def _out_projection(ya, yb, yc, yd, zg, x, mod_l, w_out_bf, ln_g, ln_b, alpha):
    b, s, d = x.shape
    tm = OUT_TM
    row3 = lambda w: pl.BlockSpec((1, tm, w), lambda i, j: (i, j, 0))
    const2 = lambda shp: pl.BlockSpec(shp, lambda i, j: (0, 0))
    return pl.pallas_call(
        functools.partial(_outproj_kernel, alpha=alpha),
        out_shape=jax.ShapeDtypeStruct((b, s, d), F32),
        grid=(b, s // tm),
        in_specs=[row3(A_W), row3(B_W),
                  pl.BlockSpec((1, yc.shape[1], tm, yc.shape[3]), lambda i, j: (i, 0, j, 0)),
                  row3(D_W), row3(4 * A_W), row3(d),
                  pl.BlockSpec((1, 1, mod_l.shape[-1]), lambda i, j: (i, 0, 0)),
                  const2(w_out_bf.shape), const2((1, d)), const2((1, d))],
        out_specs=row3(d),
        compiler_params=_params("parallel", "parallel"),
        name="out_projection",
    )(ya, yb, yc, yd, zg, x, mod_l.reshape(b, 1, -1), w_out_bf, ln_g.reshape(1, d), ln_b.reshape(1, d))


def _block_diag(w):
    g, bd, _ = w.shape
    eye = jnp.eye(g, dtype=w.dtype)
    return (eye[:, None, :, None] * w[:, :, None, :]).reshape(g * bd, g * bd)


def kernel(x, c, positions, w_mod, b_mod, w_in, conv_w, conv_b, lru_wa, lru_ba, lru_wx, lru_bx, lru_lam,
           gla_wr, gla_br, gla_gn, w_out, ln_g, ln_b):
    depth = w_mod.shape[0]
    alpha = (2 * depth) ** 0.25
    mod = _modulation(c, w_mod, b_mod)
    cos_t, sin_t = _rope_tables(positions)
    for l in range(depth):
        w_in_bf = w_in[l].astype(BF16)
        wvt_bf = w_in[l][:, OFF_AV:OFF_AV + A_W].T.astype(BF16)
        aq, ak, avt, cq, ck, cv, zg, zb, zd = _in_projection(x, mod[l], w_in_bf, wvt_bf, cos_t, sin_t,
                                                           gla_wr[l], gla_br[l])
        ya = _moba(aq, ak, avt)
        w_gates = jnp.concatenate([_block_diag(lru_wa[l]), _block_diag(lru_wx[l])], axis=1).astype(BF16)
        b_gates = jnp.concatenate([lru_ba[l], lru_bx[l]])
        yb = _rg_lru(zb, conv_w[l], conv_b[l], w_gates, b_gates, lru_lam[l])
        yc = _dilated(cq, ck, cv)
        yd = _gla(zd, gla_gn[l])
        x = _out_projection(ya, yb, yc, yd, zg, x, mod[l], w_out[l].astype(BF16), ln_g[l], ln_b[l], alpha)
    return x
```

```python
import functools

import jax
import jax.numpy as jnp
from jax import lax
from jax.experimental import pallas as pl
from jax.experimental.pallas import tpu as pltpu

F32 = jnp.float32
BF16 = jnp.bfloat16
HIGHEST = lax.Precision.HIGHEST

HEAD_DIM = 64
MOBA_HEADS = 4
MOBA_BLOCK = 256
MOBA_TOPK = 3
LRU_WIDTH = 256
LRU_BLOCKS = 4
CONV_WIDTH = 4
LRU_C = 8.0
DIL_HEADS = 4
DIL_CONFIGS = ((128, 1), (512, 4), (2048, 16))
DIL_QBLOCK = 128
GLA_HEADS = 4
GLA_DK = 32
GLA_DV = 64
GLA_LOWRANK = 16
GLA_TAU = 16.0
GLA_CHUNK = 32
ROPE_THETA = 10000.0
LN_EPS = 1e-5
NEG = -1e30
LOG2E = 1.4426950408889634
MOBA_VROWS = HEAD_DIM + 16
GLA_SAFE_DECAY = 60.0

A_W = MOBA_HEADS * HEAD_DIM
B_W = LRU_WIDTH
C_W = DIL_HEADS * HEAD_DIM
D_KW = GLA_HEADS * GLA_DK
D_W = GLA_HEADS * GLA_DV
OFF_AQ, OFF_AK, OFF_AV, OFF_AG = 0, A_W, 2 * A_W, 3 * A_W
OFF_BX = 4 * A_W
OFF_BG = OFF_BX + B_W
OFF_CQ = OFF_BG + B_W
OFF_CK, OFF_CV, OFF_CG = OFF_CQ + C_W, OFF_CQ + 2 * C_W, OFF_CQ + 3 * C_W
OFF_DQ = OFF_CQ + 4 * C_W
OFF_DK = OFF_DQ + D_KW
OFF_DV = OFF_DK + D_KW
OFF_DG = OFF_DV + D_W
OFF_DR = OFF_DG + D_W
D_IN = OFF_DR + GLA_LOWRANK

V7X_LANES = 128
V7X_SUBLANES = 8
V7X_VMEM_LIMIT_BYTES = 48 * 1024 * 1024

IN_TM = 512
OUT_TM = 1024
LRU_TL = 512
GLA_TG = 256
ROPE_TR = 512
ROPE_PACK = V7X_LANES // (HEAD_DIM // 2)
DIL_UNROLL = 2
DIL_TILE = DIL_QBLOCK * max(d for _, d in DIL_CONFIGS)


def _params(*sem):
    return pltpu.CompilerParams(dimension_semantics=sem, vmem_limit_bytes=V7X_VMEM_LIMIT_BYTES)


def _sigmoid(x):
    return 1.0 / (1.0 + jnp.exp(-x))


def _log_sigmoid(x):
    return jnp.minimum(x, 0.0) - jnp.log1p(jnp.exp(-jnp.abs(x)))


def _dot_nt(a, b, **kw):
    return lax.dot_general(a, b, (((1,), (1,)), ((), ())), preferred_element_type=F32, **kw)


def _mod_kernel(c_ref, w_ref, b_ref, o_ref):
    o_ref[0] = jnp.dot(c_ref[...], w_ref[0], preferred_element_type=F32, precision=HIGHEST) + b_ref[0]


def _modulation(c, w_mod, b_mod):
    depth, d, d3 = w_mod.shape
    b = c.shape[0]
    nt = d3 // d
    return pl.pallas_call(
        _mod_kernel,
        out_shape=jax.ShapeDtypeStruct((depth, b, d3), F32),
        grid=(depth, nt),
        in_specs=[pl.BlockSpec((b, d), lambda l, j: (0, 0)),
                  pl.BlockSpec((1, d, d), lambda l, j: (l, 0, j)),
                  pl.BlockSpec((1, 1, d), lambda l, j: (l, 0, j))],
        out_specs=pl.BlockSpec((1, b, d), lambda l, j: (l, 0, j)),
        compiler_params=_params("parallel", "parallel"),
        name="adaln_modulation",
    )(c, w_mod, b_mod.reshape(depth, 1, d3))


def _split3(x):
    hi = x.astype(BF16)
    mid = (x - hi.astype(F32)).astype(BF16)
    lo = (x - hi.astype(F32) - mid.astype(F32)).astype(BF16)
    return hi, mid, lo


def _rope_table_kernel(pos_ref, inv_ref, cos_ref, sin_ref):
    half = HEAD_DIM // 2
    rows = pos_ref.shape[1]
    ang = pos_ref[0].astype(F32) * inv_ref[...]
    src = lax.broadcasted_iota(jnp.int32, (V7X_LANES, V7X_LANES), 0)
    dst = lax.broadcasted_iota(jnp.int32, (V7X_LANES, V7X_LANES), 1)
    sign = jnp.where((dst % HEAD_DIM) < half, -1.0, 1.0)
    for table_ref, vals, signed in ((cos_ref, jnp.cos(ang), False), (sin_ref, jnp.sin(ang), True)):
        parts = _split3(vals)
        for r in range(ROPE_PACK):
            pick = src == r * half + dst % half
            spread = jnp.where(pick, sign if signed else 1.0, 0.0).astype(BF16)
            full = sum(jnp.dot(p, spread, preferred_element_type=F32) for p in parts)
            table_ref.at[0][pl.ds(r, rows, stride=ROPE_PACK), :] = full


def _rope_tables(positions):
    b, s = positions.shape
    half = HEAD_DIM // 2
    inv = ROPE_THETA ** (-jnp.arange(half, dtype=F32) / half)
    inv_row = jnp.tile(inv, ROPE_PACK).reshape(1, V7X_LANES)
    packed = jnp.repeat(positions.reshape(b, s // ROPE_PACK, ROPE_PACK), half, axis=-1)
    tab = jax.ShapeDtypeStruct((b, s, V7X_LANES), F32)
    out = pl.BlockSpec((1, ROPE_TR, V7X_LANES), lambda i, j: (i, j, 0))
    return pl.pallas_call(
        _rope_table_kernel,
        out_shape=(tab, tab),
        grid=(b, s // ROPE_TR),
        in_specs=[pl.BlockSpec((1, ROPE_TR // ROPE_PACK, V7X_LANES), lambda i, j: (i, j, 0)),
                  pl.BlockSpec((1, V7X_LANES), lambda i, j: (0, 0))],
        out_specs=(out, out),
        compiler_params=_params("parallel", "parallel"),
        name="rope_tables",
    )(packed, inv_row)


def _rope(z, cos, sin):
    lane = lax.broadcasted_iota(jnp.int32, (1, V7X_LANES), 1)
    first_half = (lane % HEAD_DIM) < (HEAD_DIM // 2)
    outs = []
    for s in range(z.shape[1] // V7X_LANES):
        zs = z[:, s * V7X_LANES:(s + 1) * V7X_LANES]
        swapped = jnp.where(first_half, pltpu.roll(zs, V7X_LANES - HEAD_DIM // 2, 1), pltpu.roll(zs, HEAD_DIM // 2, 1))
        outs.append(zs * cos + swapped * sin)
    return jnp.concatenate(outs, axis=-1)


def _inproj_kernel(x_ref, mod_ref, w_ref, wvt_ref, cos_ref, sin_ref, wr_ref, br_ref,
                   aq_ref, ak_ref, avt_ref, cq_ref, ck_ref, cv_ref, zg_ref, zb_ref, zd_ref):
    d = x_ref.shape[2]
    x = x_ref[0]
    mu = jnp.mean(x, axis=-1, keepdims=True)
    xc = x - mu
    var = jnp.mean(xc * xc, axis=-1, keepdims=True)
    xn = xc * lax.rsqrt(var + LN_EPS)
    shift = mod_ref[0, :, 0:d]
    scale = mod_ref[0, :, d:2 * d]
    u = (xn * (1.0 + scale) + shift).astype(BF16)

    def proj(off, width):
        return jnp.dot(u, w_ref[:, off:off + width], preferred_element_type=F32)

    cos = cos_ref[0]
    sin = sin_ref[0]
    q_scale = HEAD_DIM ** -0.5
    aq_ref[0] = (_rope(proj(OFF_AQ, A_W), cos, sin) * (q_scale * LOG2E)).astype(BF16)
    ak_ref[0] = _rope(proj(OFF_AK, A_W), cos, sin).astype(BF16)
    vt = _dot_nt(wvt_ref[...], u).astype(BF16)
    ones = jnp.ones((MOBA_VROWS - HEAD_DIM, vt.shape[1]), BF16)
    for h in range(MOBA_HEADS):
        avt_ref[0, h * MOBA_VROWS:h * MOBA_VROWS + HEAD_DIM, :] = vt[h * HEAD_DIM:(h + 1) * HEAD_DIM, :]
        avt_ref[0, h * MOBA_VROWS + HEAD_DIM:(h + 1) * MOBA_VROWS, :] = ones
    for ref, z in ((cq_ref, _rope(proj(OFF_CQ, C_W), cos, sin) * (q_scale * LOG2E)),
                   (ck_ref, _rope(proj(OFF_CK, C_W), cos, sin)),
                   (cv_ref, proj(OFF_CV, C_W))):
        for half in range(C_W // V7X_LANES):
            ref[0, half] = z[:, half * V7X_LANES:(half + 1) * V7X_LANES]
    for j, off in enumerate((OFF_AG, OFF_BG, OFF_CG, OFF_DG)):
        g = proj(off, A_W)
        zg_ref[0, :, j * A_W:(j + 1) * A_W] = (g * _sigmoid(g)).astype(BF16)
    zb_ref[0] = proj(OFF_BX, B_W)
    dqk = proj(OFF_DQ, 2 * D_KW)
    zd_ref[0, :, 0:D_KW] = dqk[:, 0:D_KW] * (GLA_DK ** -0.5)
    zd_ref[0, :, D_KW:2 * D_KW] = dqk[:, D_KW:]
    zd_ref[0, :, 2 * D_KW:2 * D_KW + D_W] = proj(OFF_DV, D_W)
    dr = proj(OFF_DR, GLA_LOWRANK)
    la = jnp.dot(dr, wr_ref[...], preferred_element_type=F32, precision=HIGHEST) + br_ref[...]
    zd_ref[0, :, 2 * D_KW + D_W:] = _log_sigmoid(la) * (1.0 / GLA_TAU)


def _in_projection(x, mod_l, w_in_bf, wvt_bf, layer, cos_t, sin_t, gla_wr, gla_br):
    b, s, d = x.shape
    tm = IN_TM
    row3 = lambda w: pl.BlockSpec((1, tm, w), lambda i, j: (i, j, 0))
    const2 = lambda shp: pl.BlockSpec(shp, lambda i, j: (0, 0))
    per_layer = lambda shp: pl.BlockSpec((None,) + shp[1:], lambda i, j: (layer, 0, 0))
    halves = pl.BlockSpec((1, C_W // V7X_LANES, tm, V7X_LANES), lambda i, j: (i, 0, j, 0))
    out_shape = (
        jax.ShapeDtypeStruct((b, s, A_W), BF16),
        jax.ShapeDtypeStruct((b, s, A_W), BF16),
        jax.ShapeDtypeStruct((b, MOBA_HEADS * MOBA_VROWS, s), BF16),
        jax.ShapeDtypeStruct((b, C_W // V7X_LANES, s, V7X_LANES), F32),
        jax.ShapeDtypeStruct((b, C_W // V7X_LANES, s, V7X_LANES), F32),
        jax.ShapeDtypeStruct((b, C_W // V7X_LANES, s, V7X_LANES), F32),
        jax.ShapeDtypeStruct((b, s, 4 * A_W), BF16),
        jax.ShapeDtypeStruct((b, s, B_W), F32),
        jax.ShapeDtypeStruct((b, s, 2 * D_KW + D_W + D_KW), F32),
    )
    out_specs = (row3(A_W), row3(A_W), pl.BlockSpec((1, MOBA_HEADS * MOBA_VROWS, tm), lambda i, j: (i, 0, j)),
                 halves, halves, halves, row3(4 * A_W), row3(B_W), row3(2 * D_KW + D_W + D_KW))
    return pl.pallas_call(
        _inproj_kernel,
        out_shape=out_shape,
        grid=(b, s // tm),
        in_specs=[row3(d),
                  pl.BlockSpec((1, 1, mod_l.shape[-1]), lambda i, j: (i, 0, 0)),
                  per_layer(w_in_bf.shape), per_layer(wvt_bf.shape),
                  row3(V7X_LANES), row3(V7X_LANES),
                  const2(gla_wr.shape), const2((1, D_KW))],
        out_specs=out_specs,
        compiler_params=_params("parallel", "parallel"),
        name="in_projection",
    )(x, mod_l.reshape(b, 1, -1), w_in_bf, wvt_bf, cos_t, sin_t, gla_wr, gla_br.reshape(1, D_KW))


def _moba_kernel(q_ref, k_ref, vt_ref, o_ref, kmean_ref, qm_ref, unsel_ref, s_ref, p_ref, alpha_ref, acc_ref):
    blk = MOBA_BLOCK
    nb = k_ref.shape[1] // blk
    n = pl.program_id(1)

    @pl.when(n == 0)
    def _():
        for j in range(nb):
            kj = k_ref[0, j * blk:(j + 1) * blk, :].astype(F32)
            kmean_ref[j:j + 1, :] = jnp.mean(kj, axis=0, keepdims=True)

    q = q_ref[0]
    lane_head = lax.broadcasted_iota(jnp.int32, (1, A_W), 1) // HEAD_DIM
    km = kmean_ref[...]
    km_hi = km.astype(BF16)
    km_lo = (km - km_hi.astype(F32)).astype(BF16)
    key_idx = lax.broadcasted_iota(jnp.int32, (blk, blk), 0)
    qry_idx = lax.broadcasted_iota(jnp.int32, (blk, blk), 1)
    causal = key_idx <= qry_idx

    vr = MOBA_VROWS
    vth = lambda h, j: vt_ref[0, h * vr:(h + 1) * vr, pl.ds(pl.multiple_of(j * blk, blk), blk)]
    kblk = lambda j: k_ref[0, pl.ds(pl.multiple_of(j * blk, blk), blk), :]

    for h in range(MOBA_HEADS):
        qm_ref[h * blk:(h + 1) * blk, :] = jnp.where(lane_head == h, q, jnp.zeros_like(q))
    scores = lambda j: _dot_nt(kblk(j), qm_ref[...])

    blk_id = lax.broadcasted_iota(jnp.int32, (nb, MOBA_HEADS * blk), 0)
    own = _dot_nt(jnp.concatenate([kblk(n), km_hi, km_lo], axis=0), qm_ref[...])
    s_own = own[0:blk, :]
    gate = jnp.where(blk_id < n, own[blk:blk + nb, :] + own[blk + nb:blk + 2 * nb, :], NEG)
    s_ref[0] = scores(0)
    rank = jnp.zeros(gate.shape, jnp.int32)
    for i in range(nb):
        gi = gate[i:i + 1, :]
        tie = (blk_id > i).astype(jnp.int32)
        rank = rank + jnp.where(gi > gate, 1, jnp.where(gi == gate, tie, 0))
    unsel_ref[...] = jnp.where((rank < MOBA_TOPK) & (blk_id < n), 0.0, 1.0)

    def accumulate(slot, j_blk):
        for h in range(MOBA_HEADS):
            cols = slice(h * blk, (h + 1) * blk)
            acc_ref[h * vr:(h + 1) * vr, :] = (alpha_ref[slot, :, cols] * acc_ref[h * vr:(h + 1) * vr, :]
                                              + jnp.dot(vth(h, j_blk), p_ref[slot, :, cols],
                                                        preferred_element_type=F32))

    acc_ref[...] = jnp.zeros_like(acc_ref)
    alpha_ref[1] = jnp.ones(alpha_ref.shape[1:], F32)
    s = jnp.where(jnp.concatenate([causal] * MOBA_HEADS, axis=1), s_own, NEG)
    m_init = jnp.max(s, axis=0, keepdims=True)
    p_ref[1] = jnp.exp2(s - m_init).astype(BF16)

    def step(j, j_prev, slot, m):
        s = s_ref[slot]
        accumulate(1 - slot, j_prev)
        s_ref[1 - slot] = scores(jnp.minimum(j + 1, nb - 1))
        unsel = unsel_ref[pl.ds(j, 1), :] > 0.5
        m_new = jnp.maximum(m, jnp.where(unsel, NEG, jnp.max(s, axis=0, keepdims=True)))
        alpha_ref[slot] = jnp.exp2(m - m_new)
        p_ref[slot] = jnp.exp2(s - jnp.where(unsel, -NEG, m_new)).astype(BF16)
        return m_new

    def body(i, m):
        m = step(2 * i, jnp.where(i == 0, n, 2 * i - 1), 0, m)
        return step(2 * i + 1, 2 * i, 1, m)

    pairs = (n + 1) // 2
    lax.fori_loop(0, pairs, body, m_init)
    accumulate(1, jnp.where(pairs == 0, n, 2 * pairs - 1))
    outs = []
    for h in range(MOBA_HEADS):
        a = acc_ref[h * vr:(h + 1) * vr, :]
        outs.append(a[0:HEAD_DIM, :] / a[HEAD_DIM:HEAD_DIM + 1, :])
    o_ref[0] = jnp.concatenate(outs, axis=0).T.astype(o_ref.dtype)


def _moba(aq, ak, avt):
    b, s, _ = aq.shape
    blk = MOBA_BLOCK
    return pl.pallas_call(
        _moba_kernel,
        out_shape=jax.ShapeDtypeStruct((b, s, A_W), BF16),
        grid=(b, s // blk),
        in_specs=[pl.BlockSpec((1, blk, A_W), lambda i, n: (i, n, 0)),
                  pl.BlockSpec((1, s, A_W), lambda i, n: (i, 0, 0)),
                  pl.BlockSpec((1, MOBA_HEADS * MOBA_VROWS, s), lambda i, n: (i, 0, 0))],
        out_specs=pl.BlockSpec((1, blk, A_W), lambda i, n: (i, n, 0)),
        scratch_shapes=[pltpu.VMEM((s // blk, A_W), F32),
                        pltpu.VMEM((MOBA_HEADS * blk, A_W), BF16),
                        pltpu.VMEM((s // blk, MOBA_HEADS * blk), F32),
                        pltpu.VMEM((2, blk, MOBA_HEADS * blk), F32),
                        pltpu.VMEM((2, blk, MOBA_HEADS * blk), BF16),
                        pltpu.VMEM((2, 1, MOBA_HEADS * blk), F32),
                        pltpu.VMEM((MOBA_HEADS * MOBA_VROWS, blk), F32)],
        compiler_params=_params("arbitrary", "arbitrary"),
        name="moba_attention",
    )(aq, ak, avt)


def _lru_kernel(x_ref, cw_ref, cb_ref, wg_ref, bg_ref, lam_ref, o_ref, xs_ref, h_ref):
    tl = x_ref.shape[1]
    halo = V7X_SUBLANES
    i = pl.program_id(1)

    @pl.when(i == 0)
    def _():
        xs_ref[0:halo, :] = jnp.zeros((halo, B_W), F32)
        h_ref[...] = jnp.zeros_like(h_ref)

    x = x_ref[0]
    xs_ref[halo:, :] = x
    y = cb_ref[...] + cw_ref[CONV_WIDTH - 1:CONV_WIDTH, :] * x
    for k in range(CONV_WIDTH - 1):
        y = y + cw_ref[k:k + 1, :] * xs_ref[pl.ds(halo - (CONV_WIDTH - 1) + k, tl), :]
    xs_ref[0:halo, :] = x[tl - halo:, :]

    gates = jnp.dot(y.astype(BF16), wg_ref[...], preferred_element_type=F32) + bg_ref[...]
    r = _sigmoid(gates[:, 0:B_W])
    ig = _sigmoid(gates[:, B_W:])
    lam = lam_ref[...]
    softplus_neg_lam = jnp.maximum(-lam, 0.0) + jnp.log1p(jnp.exp(-jnp.abs(lam)))
    log_a = (-LRU_C) * r * softplus_neg_lam
    a = jnp.exp(log_a)
    u = jnp.sqrt(-jnp.tanh(log_a) * (a * a + 1.0)) * (ig * y)

    row = lax.broadcasted_iota(jnp.int32, (tl, 1), 0)
    step = 1
    while step < tl:
        keep = row >= step
        a_prev = jnp.where(keep, pltpu.roll(a, step, 0), 1.0)
        u_prev = jnp.where(keep, pltpu.roll(u, step, 0), 0.0)
        u = u + a * u_prev
        a = a * a_prev
        step *= 2
    h = u + a * h_ref[...]
    h_ref[...] = h[tl - 1:tl, :]
    o_ref[0] = h.astype(o_ref.dtype)


def _rg_lru(zb, conv_w, conv_b, w_gates, b_gates, lam):
    b, s, w = zb.shape
    tl = LRU_TL
    const = lambda shp: pl.BlockSpec(shp, lambda i, j: (0, 0))
    return pl.pallas_call(
        _lru_kernel,
        out_shape=jax.ShapeDtypeStruct((b, s, w), BF16),
        grid=(b, s // tl),
        in_specs=[pl.BlockSpec((1, tl, w), lambda i, j: (i, j, 0)),
                  const(conv_w.shape), const((1, w)), const(w_gates.shape), const((1, 2 * w)), const((1, w))],
        out_specs=pl.BlockSpec((1, tl, w), lambda i, j: (i, j, 0)),
        scratch_shapes=[pltpu.VMEM((tl + V7X_SUBLANES, w), F32), pltpu.VMEM((1, w), F32)],
        compiler_params=_params("arbitrary", "arbitrary"),
        name="rg_lru",
    )(zb, conv_w, conv_b.reshape(1, w), w_gates, b_gates.reshape(1, 2 * w), lam.reshape(1, w))


def _dilated_kernel(q_ref, k_ref, v_ref, o_ref, oacc_ref, lacc_ref):
    qb = DIL_QBLOCK
    tile = o_ref.shape[1]
    t0 = pl.program_id(1) * tile
    halves = range(C_W // V7X_LANES)

    def load(ref, rows):
        return jnp.concatenate([ref.at[half][rows, :] for half in halves], axis=-1)

    def store(ref, rows, value):
        for half in halves:
            ref.at[half][rows, :] = value[:, half * V7X_LANES:(half + 1) * V7X_LANES]

    lane_head = lax.broadcasted_iota(jnp.int32, (1, C_W), 1) // HEAD_DIM
    qi = lax.broadcasted_iota(jnp.int32, (qb, 2 * qb), 0)
    kj = lax.broadcasted_iota(jnp.int32, (qb, 2 * qb), 1)

    for ci, (window, d) in enumerate(DIL_CONFIGS):
        n_steps = window // d
        per_class = tile // (qb * d)
        first, last = ci == 0, ci == len(DIL_CONFIGS) - 1

        def units(i, carry, d=d, n_steps=n_steps, per_class=per_class, first=first, last=last):
            group = []
            for k in range(DIL_UNROLL):
                u = i * DIL_UNROLL + k
                r = u // per_class
                base = t0 // d + (u % per_class) * qb
                start = jnp.maximum(base - qb, 0)
                if d == 1:
                    tokens = lambda m0, count, r=r: pl.ds(pl.multiple_of(m0, qb), count)
                else:
                    tokens = lambda m0, count, r=r: pl.ds(m0 * d + r, count, stride=d)
                q = load(q_ref, tokens(base, qb))
                kk = load(k_ref, tokens(start, 2 * qb)).astype(BF16)
                vv = load(v_ref, tokens(start, 2 * qb)).astype(BF16)
                dist = (base - start) + qi - kj
                valid = (dist >= 0) & (dist <= n_steps)
                q_stack = jnp.concatenate([jnp.where(lane_head == h, q, 0.0) for h in range(DIL_HEADS)], axis=0)
                s = _dot_nt(q_stack.astype(BF16), kk)
                group.append((s, valid, vv, tokens(base - t0 // d, qb)))
            soft = []
            for s, valid, vv, local in group:
                s = jnp.where(jnp.concatenate([valid] * DIL_HEADS, axis=0), s, NEG)
                m = jnp.max(s, axis=-1, keepdims=True)
                e = jnp.exp2(s - m)
                soft.append((e.astype(BF16), m, jnp.sum(e, axis=-1, keepdims=True)))
            pvs = [jnp.dot(e, vv, preferred_element_type=F32)
                   for (e, _, _), (_, _, vv, _) in zip(soft, group)]
            for pv, (_, m, l), (_, _, _, local) in zip(pvs, soft, group):
                o_all = pv / l
                lse_all = m + jnp.log2(l)
                out = o_all[0:qb, :]
                lse = jnp.broadcast_to(lse_all[0:qb, :], (qb, C_W))
                for h in range(1, DIL_HEADS):
                    out = jnp.where(lane_head == h, o_all[h * qb:(h + 1) * qb, :], out)
                    lse = jnp.where(lane_head == h, lse_all[h * qb:(h + 1) * qb, :], lse)
                if not first:
                    o_prev = load(oacc_ref, local)
                    l_prev = load(lacc_ref, local)
                    top = jnp.maximum(l_prev, lse)
                    w_prev = jnp.exp2(l_prev - top)
                    w_cur = jnp.exp2(lse - top)
                    out = (w_prev * o_prev + w_cur * out) / (w_prev + w_cur)
                    lse = top + jnp.log2(w_prev + w_cur)
                if last:
                    store(o_ref, local, out)
                else:
                    store(oacc_ref, local, out)
                    store(lacc_ref, local, lse)
            return carry

        lax.fori_loop(0, tile // (qb * DIL_UNROLL), units, 0)


def _dilated(cq, ck, cv):
    b, nh, s, w = cq.shape
    tile = DIL_TILE
    whole = pl.BlockSpec((None, nh, s, w), lambda i, j: (i, 0, 0, 0))
    return pl.pallas_call(
        _dilated_kernel,
        out_shape=jax.ShapeDtypeStruct((b, nh, s, w), F32),
        grid=(b, s // tile),
        in_specs=[whole, whole, whole],
        out_specs=pl.BlockSpec((None, nh, tile, w), lambda i, j: (i, 0, j, 0)),
        scratch_shapes=[pltpu.VMEM((nh, tile, w), F32), pltpu.VMEM((nh, tile, w), F32)],
        compiler_params=_params("parallel", "arbitrary"),
        name="dilated_attention",
    )(cq, ck, cv)


def _gla_kernel(z_ref, gn_ref, o_ref, state_ref, b_ref, q_ref, k_ref, v_ref, t_ref, y_ref):
    c = GLA_CHUNK
    tg = z_ref.shape[1]
    nch = tg // c
    i = pl.program_id(1)

    @pl.when(i == 0)
    def _():
        state_ref[...] = jnp.zeros_like(state_ref)

    qs = z_ref[0, :, 0:D_KW]
    ks = z_ref[0, :, D_KW:2 * D_KW]
    vs = z_ref[0, :, 2 * D_KW:2 * D_KW + D_W]
    la = z_ref[0, :, 2 * D_KW + D_W:]

    ri = lax.broadcasted_iota(jnp.int32, (tg, tg), 0)
    ci = lax.broadcasted_iota(jnp.int32, (tg, tg), 1)
    chunk_causal = (ri // c == ci // c) & (ci <= ri)
    tri = jnp.where(chunk_causal, 1.0, 0.0).astype(BF16)
    b = sum(jnp.dot(tri, part, preferred_element_type=F32) for part in _split3(la))
    b_last = jnp.broadcast_to(b.reshape(nch, c, D_KW)[:, c - 1:c, :], (nch, c, D_KW)).reshape(tg, D_KW)
    q_in = qs * jnp.exp(b)
    k_dec = ks * jnp.exp(b_last - b)
    v_bf = vs.astype(BF16)

    safe = jnp.min(la) * c > -GLA_SAFE_DECAY

    @pl.when(safe)
    def _():
        k_grow = (ks * jnp.exp(-b)).astype(BF16)
        dk_head = lax.broadcasted_iota(jnp.int32, (1, D_KW), 1) // GLA_DK
        q_stack = jnp.concatenate([jnp.where(dk_head == h, q_in, 0.0) for h in range(GLA_HEADS)], axis=0)
        a = _dot_nt(q_stack.astype(BF16), k_grow)
        a = jnp.where(jnp.concatenate([chunk_causal] * GLA_HEADS, axis=0), a, 0.0)
        av = jnp.dot(a.astype(BF16), v_bf, preferred_element_type=F32)
        dv_head = lax.broadcasted_iota(jnp.int32, (1, D_W), 1) // GLA_DV
        y = av[0:tg, :]
        for h in range(1, GLA_HEADS):
            y = jnp.where(dv_head == h, av[h * tg:(h + 1) * tg, :], y)
        y_ref[...] = y

    @pl.when(jnp.logical_not(safe))
    def _():
        b_ref[...] = b
        q_ref[...] = qs
        k_ref[...] = ks
        v_ref[...] = vs
        head_sum = (lax.broadcasted_iota(jnp.int32, (D_KW, D_W), 0) // GLA_DK
                    == lax.broadcasted_iota(jnp.int32, (D_KW, D_W), 1) // GLA_DV).astype(BF16)
        jrow = lax.broadcasted_iota(jnp.int32, (c, D_KW), 0)

        def chunk(ch, carry):
            r0 = pl.multiple_of(ch * c, c)
            bch = b_ref[pl.ds(r0, c), :]
            kch = k_ref[pl.ds(r0, c), :]

            def row_terms(ii, carry2):
                bi = b_ref[pl.ds(r0 + ii, 1), :]
                qi = q_ref[pl.ds(r0 + ii, 1), :]
                t = jnp.exp(jnp.minimum(bi - bch, 0.0)) * kch * qi
                t_ref[pl.ds(pl.multiple_of(ii * c, c), c), :] = jnp.where(jrow <= ii, t, 0.0)
                return carry2

            lax.fori_loop(0, c, row_terms, 0)
            a_b = jnp.dot(t_ref[...].astype(BF16), head_sum, preferred_element_type=F32)
            y_ref[pl.ds(r0, c), :] = jnp.sum(a_b.reshape(c, c, D_W) * v_ref[pl.ds(r0, c), :][None, :, :], axis=1)
            return carry

        lax.fori_loop(0, nch, chunk, 0)

    same_head_t = (lax.broadcasted_iota(jnp.int32, (D_W, D_KW), 0) // GLA_DV
                   == lax.broadcasted_iota(jnp.int32, (D_W, D_KW), 1) // GLA_DK)
    q_in_bf = q_in.astype(BF16)
    k_dec_bf = k_dec.astype(BF16)
    st = state_ref[...]
    inter = []
    for ch in range(nch):
        r0 = ch * c
        inter.append(_dot_nt(q_in_bf[r0:r0 + c, :], st.astype(BF16)))
        kvt = lax.dot_general(v_bf[r0:r0 + c, :], k_dec_bf[r0:r0 + c, :], (((0,), (0,)), ((), ())),
                              preferred_element_type=F32)
        st = st * jnp.exp(b_last[r0:r0 + 1, :]) + jnp.where(same_head_t, kvt, 0.0)
    state_ref[...] = st
    y = y_ref[...] + jnp.concatenate(inter, axis=0)

    e_head = jnp.where(lax.broadcasted_iota(jnp.int32, (D_W, D_W), 0) // GLA_DV
                       == lax.broadcasted_iota(jnp.int32, (D_W, D_W), 1) // GLA_DV, 1.0 / GLA_DV, 0.0).astype(BF16)
    yy = y * y
    yy_hi = yy.astype(BF16)
    yy_lo = (yy - yy_hi.astype(F32)).astype(BF16)
    ms = (jnp.dot(yy_hi, e_head, preferred_element_type=F32) + jnp.dot(yy_lo, e_head, preferred_element_type=F32))
    o_ref[0] = (y * lax.rsqrt(ms + LN_EPS) * gn_ref[...]).astype(o_ref.dtype)


def _gla(zd, gla_gn):
    b, s, w = zd.shape
    tg = GLA_TG
    gn_row = jnp.tile(gla_gn.astype(F32), GLA_HEADS).reshape(1, D_W)
    return pl.pallas_call(
        _gla_kernel,
        out_shape=jax.ShapeDtypeStruct((b, s, D_W), BF16),
        grid=(b, s // tg),
        in_specs=[pl.BlockSpec((1, tg, w), lambda i, j: (i, j, 0)),
                  pl.BlockSpec((1, D_W), lambda i, j: (0, 0))],
        out_specs=pl.BlockSpec((1, tg, D_W), lambda i, j: (i, j, 0)),
        scratch_shapes=[pltpu.VMEM((D_W, D_KW), F32),
                        pltpu.VMEM((tg, D_KW), F32),
                        pltpu.VMEM((tg, D_KW), F32),
                        pltpu.VMEM((tg, D_KW), F32),
                        pltpu.VMEM((tg, D_W), F32),
                        pltpu.VMEM((GLA_CHUNK * GLA_CHUNK, D_KW), F32),
                        pltpu.VMEM((tg, D_W), F32)],
        compiler_params=_params("arbitrary", "arbitrary"),
        name="gla",
    )(zd, gn_row)


def _outproj_kernel(ya_ref, yb_ref, yc_ref, yd_ref, zg_ref, x_ref, mod_ref, w_ref, g_ref, b_ref, o_ref, *, alpha):
    d = x_ref.shape[2]
    y = jnp.zeros((x_ref.shape[1], d), F32)
    yc = jnp.concatenate([yc_ref[0, half] for half in range(yc_ref.shape[1])], axis=-1)
    for j, yj in enumerate((ya_ref[0], yb_ref[0], yc, yd_ref[0])):
        mix = (yj.astype(F32) * zg_ref[0, :, j * A_W:(j + 1) * A_W].astype(F32)).astype(BF16)
        y = y + jnp.dot(mix, w_ref[j * A_W:(j + 1) * A_W, :], preferred_element_type=F32)
    gate = mod_ref[0, :, 2 * d:3 * d]
    r = alpha * x_ref[0] + (1.0 + gate) * y
    mu = jnp.mean(r, axis=-1, keepdims=True)
    rc = r - mu
    var = jnp.mean(rc * rc, axis=-1, keepdims=True)
    o_ref[0] = rc * lax.rsqrt(var + LN_EPS) * g_ref[...] + b_ref[...]


def _out_projection(ya, yb, yc, yd, zg, x, mod_l, w_out_bf, layer, ln_g, ln_b, alpha):
    b, s, d = x.shape
    tm = OUT_TM
    row3 = lambda w: pl.BlockSpec((1, tm, w), lambda i, j: (i, j, 0))
    const2 = lambda shp: pl.BlockSpec(shp, lambda i, j: (0, 0))
    return pl.pallas_call(
        functools.partial(_outproj_kernel, alpha=alpha),
        out_shape=jax.ShapeDtypeStruct((b, s, d), F32),
        grid=(b, s // tm),
        in_specs=[row3(A_W), row3(B_W),
                  pl.BlockSpec((1, yc.shape[1], tm, yc.shape[3]), lambda i, j: (i, 0, j, 0)),
                  row3(D_W), row3(4 * A_W), row3(d),
                  pl.BlockSpec((1, 1, mod_l.shape[-1]), lambda i, j: (i, 0, 0)),
                  pl.BlockSpec((None,) + w_out_bf.shape[1:], lambda i, j: (layer, 0, 0)),
                  const2((1, d)), const2((1, d))],
        out_specs=row3(d),
        compiler_params=_params("parallel", "parallel"),
        name="out_projection",
    )(ya, yb, yc, yd, zg, x, mod_l.reshape(b, 1, -1), w_out_bf, ln_g.reshape(1, d), ln_b.reshape(1, d))


def _block_diag(w):
    g, bd, _ = w.shape
    eye = jnp.eye(g, dtype=w.dtype)
    return (eye[:, None, :, None] * w[:, :, None, :]).reshape(g * bd, g * bd)


def kernel(x, c, positions, w_mod, b_mod, w_in, conv_w, conv_b, lru_wa, lru_ba, lru_wx, lru_bx, lru_lam,
           gla_wr, gla_br, gla_gn, w_out, ln_g, ln_b):
    depth = w_mod.shape[0]
    alpha = (2 * depth) ** 0.25
    mod = _modulation(c, w_mod, b_mod)
    cos_t, sin_t = _rope_tables(positions)
    w_in_bf = w_in.astype(BF16)
    wvt_bf = jnp.swapaxes(w_in[:, :, OFF_AV:OFF_AV + A_W], 1, 2).astype(BF16)
    w_out_bf = w_out.astype(BF16)
    for l in range(depth):
        aq, ak, avt, cq, ck, cv, zg, zb, zd = _in_projection(x, mod[l], w_in_bf, wvt_bf, l, cos_t, sin_t,
                                                               gla_wr[l], gla_br[l])
        ya = _moba(aq, ak, avt)
        w_gates = jnp.concatenate([_block_diag(lru_wa[l]), _block_diag(lru_wx[l])], axis=1).astype(BF16)
        b_gates = jnp.concatenate([lru_ba[l], lru_bx[l]])
        yb = _rg_lru(zb, conv_w[l], conv_b[l], w_gates, b_gates, lru_lam[l])
        yc = _dilated(cq, ck, cv)
        yd = _gla(zd, gla_gn[l])
        x = _out_projection(ya, yb, yc, yd, zg, x, mod[l], w_out_bf, l, ln_g[l], ln_b[l], alpha)
    return x
```

```python
import functools

import jax
import jax.numpy as jnp
from jax import lax
from jax.experimental import pallas as pl
from jax.experimental.pallas import tpu as pltpu

F32 = jnp.float32
BF16 = jnp.bfloat16

HEAD_DIM = 64
MOBA_HEADS = 4
MOBA_BLOCK = 256
MOBA_TOPK = 3
LRU_WIDTH = 256
LRU_BLOCKS = 4
CONV_WIDTH = 4
LRU_C = 8.0
DIL_HEADS = 4
DIL_CONFIGS = ((128, 1), (512, 4), (2048, 16))
DIL_QBLOCK = 128
GLA_HEADS = 4
GLA_DK = 32
GLA_DV = 64
GLA_LOWRANK = 16
GLA_TAU = 16.0
GLA_CHUNK = 32
ROPE_THETA = 10000.0
LN_EPS = 1e-5
NEG = -1e30
LOG2E = 1.4426950408889634
MOBA_VROWS = HEAD_DIM + 16
GLA_SAFE_DECAY = 60.0
GLA_FAST_CHUNK = 128

A_W = MOBA_HEADS * HEAD_DIM
B_W = LRU_WIDTH
C_W = DIL_HEADS * HEAD_DIM
D_KW = GLA_HEADS * GLA_DK
D_W = GLA_HEADS * GLA_DV
OFF_AQ, OFF_AK, OFF_AV, OFF_AG = 0, A_W, 2 * A_W, 3 * A_W
OFF_BX = 4 * A_W
OFF_BG = OFF_BX + B_W
OFF_CQ = OFF_BG + B_W
OFF_CK, OFF_CV, OFF_CG = OFF_CQ + C_W, OFF_CQ + 2 * C_W, OFF_CQ + 3 * C_W
OFF_DQ = OFF_CQ + 4 * C_W
OFF_DK = OFF_DQ + D_KW
OFF_DV = OFF_DK + D_KW
OFF_DG = OFF_DV + D_W
OFF_DR = OFF_DG + D_W
D_IN = OFF_DR + GLA_LOWRANK

V7X_LANES = 128
V7X_SUBLANES = 8
V7X_VMEM_LIMIT_BYTES = 48 * 1024 * 1024

IN_TM = 512
OUT_TM = 1024
LRU_TL = 512
GLA_TG = 256
ROPE_TR = 512
ROPE_PACK = V7X_LANES // (HEAD_DIM // 2)
DIL_UNROLL = 2
DIL_TILE = DIL_QBLOCK * max(d for _, d in DIL_CONFIGS)


def _params(*sem):
    return pltpu.CompilerParams(dimension_semantics=sem, vmem_limit_bytes=V7X_VMEM_LIMIT_BYTES)


def _sigmoid(x):
    return 1.0 / (1.0 + jnp.exp(-x))


def _log_sigmoid(x):
    return jnp.minimum(x, 0.0) - jnp.log1p(jnp.exp(-jnp.abs(x)))


def _dot_nt(a, b, **kw):
    return lax.dot_general(a, b, (((1,), (1,)), ((), ())), preferred_element_type=F32, **kw)


def _mod_kernel(c_ref, w_ref, b_ref, o_ref):
    o_ref[0] = jnp.dot(c_ref[...].astype(BF16), w_ref[0].astype(BF16), preferred_element_type=F32) + b_ref[0]


def _modulation(c, w_mod, b_mod):
    depth, d, d3 = w_mod.shape
    b = c.shape[0]
    nt = d3 // d
    return pl.pallas_call(
        _mod_kernel,
        out_shape=jax.ShapeDtypeStruct((depth, b, d3), F32),
        grid=(depth, nt),
        in_specs=[pl.BlockSpec((b, d), lambda l, j: (0, 0)),
                  pl.BlockSpec((1, d, d), lambda l, j: (l, 0, j)),
                  pl.BlockSpec((1, 1, d), lambda l, j: (l, 0, j))],
        out_specs=pl.BlockSpec((1, b, d), lambda l, j: (l, 0, j)),
        compiler_params=_params("parallel", "parallel"),
        name="adaln_modulation",
    )(c, w_mod, b_mod.reshape(depth, 1, d3))


def _split3(x):
    hi = x.astype(BF16)
    mid = (x - hi.astype(F32)).astype(BF16)
    lo = (x - hi.astype(F32) - mid.astype(F32)).astype(BF16)
    return hi, mid, lo


def _rope_table_kernel(pos_ref, inv_ref, cos_ref, sin_ref):
    half = HEAD_DIM // 2
    rows = pos_ref.shape[1]
    ang = pos_ref[0].astype(F32) * inv_ref[...]
    src = lax.broadcasted_iota(jnp.int32, (V7X_LANES, V7X_LANES), 0)
    dst = lax.broadcasted_iota(jnp.int32, (V7X_LANES, V7X_LANES), 1)
    sign = jnp.where((dst % HEAD_DIM) < half, -1.0, 1.0)
    for table_ref, vals, signed in ((cos_ref, jnp.cos(ang), False), (sin_ref, jnp.sin(ang), True)):
        parts = _split3(vals)
        for r in range(ROPE_PACK):
            pick = src == r * half + dst % half
            spread = jnp.where(pick, sign if signed else 1.0, 0.0).astype(BF16)
            full = sum(jnp.dot(p, spread, preferred_element_type=F32) for p in parts)
            table_ref.at[0][pl.ds(r, rows, stride=ROPE_PACK), :] = full


def _rope_tables(positions):
    b, s = positions.shape
    half = HEAD_DIM // 2
    inv = ROPE_THETA ** (-jnp.arange(half, dtype=F32) / half)
    inv_row = jnp.tile(inv, ROPE_PACK).reshape(1, V7X_LANES)
    packed = jnp.repeat(positions.reshape(b, s // ROPE_PACK, ROPE_PACK), half, axis=-1)
    tab = jax.ShapeDtypeStruct((b, s, V7X_LANES), F32)
    out = pl.BlockSpec((1, ROPE_TR, V7X_LANES), lambda i, j: (i, j, 0))
    return pl.pallas_call(
        _rope_table_kernel,
        out_shape=(tab, tab),
        grid=(b, s // ROPE_TR),
        in_specs=[pl.BlockSpec((1, ROPE_TR // ROPE_PACK, V7X_LANES), lambda i, j: (i, j, 0)),
                  pl.BlockSpec((1, V7X_LANES), lambda i, j: (0, 0))],
        out_specs=(out, out),
        compiler_params=_params("parallel", "parallel"),
        name="rope_tables",
    )(packed, inv_row)


def _rope(z, cos, sin):
    lane = lax.broadcasted_iota(jnp.int32, (1, V7X_LANES), 1)
    first_half = (lane % HEAD_DIM) < (HEAD_DIM // 2)
    outs = []
    for s in range(z.shape[1] // V7X_LANES):
        zs = z[:, s * V7X_LANES:(s + 1) * V7X_LANES]
        swapped = jnp.where(first_half, pltpu.roll(zs, V7X_LANES - HEAD_DIM // 2, 1), pltpu.roll(zs, HEAD_DIM // 2, 1))
        outs.append(zs * cos + swapped * sin)
    return jnp.concatenate(outs, axis=-1)


def _inproj_kernel(x_ref, mod_ref, w_ref, cos_ref, sin_ref, wr_ref, br_ref,
                   aq_ref, ak_ref, avt_ref, cq_ref, ck_ref, cv_ref, zg_ref, zb_ref, zd_ref):
    d = x_ref.shape[2]
    x = x_ref[0]
    mu = jnp.mean(x, axis=-1, keepdims=True)
    xc = x - mu
    var = jnp.mean(xc * xc, axis=-1, keepdims=True)
    xn = xc * lax.rsqrt(var + LN_EPS)
    shift = mod_ref[0, :, 0:d]
    scale = mod_ref[0, :, d:2 * d]
    u = (xn * (1.0 + scale) + shift).astype(BF16)

    def proj(off, width):
        return jnp.dot(u, w_ref[:, off:off + width], preferred_element_type=F32)

    cos = cos_ref[0]
    sin = sin_ref[0]
    q_scale = HEAD_DIM ** -0.5
    aq_ref[0] = (_rope(proj(OFF_AQ, A_W), cos, sin) * (q_scale * LOG2E)).astype(BF16)
    ak_ref[0] = _rope(proj(OFF_AK, A_W), cos, sin).astype(BF16)
    vt = proj(OFF_AV, A_W).T.astype(BF16)
    ones = jnp.ones((MOBA_VROWS - HEAD_DIM, vt.shape[1]), BF16)
    for h in range(MOBA_HEADS):
        avt_ref[0, h * MOBA_VROWS:h * MOBA_VROWS + HEAD_DIM, :] = vt[h * HEAD_DIM:(h + 1) * HEAD_DIM, :]
        avt_ref[0, h * MOBA_VROWS + HEAD_DIM:(h + 1) * MOBA_VROWS, :] = ones
    for ref, z in ((cq_ref, _rope(proj(OFF_CQ, C_W), cos, sin) * (q_scale * LOG2E)),
                   (ck_ref, _rope(proj(OFF_CK, C_W), cos, sin)),
                   (cv_ref, proj(OFF_CV, C_W))):
        for half in range(C_W // V7X_LANES):
            ref[0, half] = z[:, half * V7X_LANES:(half + 1) * V7X_LANES]
    for j, off in enumerate((OFF_AG, OFF_BG, OFF_CG, OFF_DG)):
        g = proj(off, A_W)
        zg_ref[0, :, j * A_W:(j + 1) * A_W] = (g * _sigmoid(g)).astype(BF16)
    zb_ref[0] = proj(OFF_BX, B_W)
    dqk = proj(OFF_DQ, 2 * D_KW)
    zd_ref[0, :, 0:D_KW] = dqk[:, 0:D_KW] * (GLA_DK ** -0.5)
    zd_ref[0, :, D_KW:2 * D_KW] = dqk[:, D_KW:]
    zd_ref[0, :, 2 * D_KW:2 * D_KW + D_W] = proj(OFF_DV, D_W)
    dr = proj(OFF_DR, GLA_LOWRANK)
    la = jnp.dot(dr.astype(BF16), wr_ref[...].astype(BF16), preferred_element_type=F32) + br_ref[...]
    zd_ref[0, :, 2 * D_KW + D_W:] = _log_sigmoid(la) * (1.0 / GLA_TAU)


def _in_projection(x, mod_l, w_in_bf, layer, cos_t, sin_t, gla_wr, gla_br):
    b, s, d = x.shape
    tm = IN_TM
    row3 = lambda w: pl.BlockSpec((1, tm, w), lambda i, j: (i, j, 0))
    const2 = lambda shp: pl.BlockSpec(shp, lambda i, j: (0, 0))
    per_layer = lambda shp: pl.BlockSpec((None,) + shp[1:], lambda i, j: (layer, 0, 0))
    halves = pl.BlockSpec((1, C_W // V7X_LANES, tm, V7X_LANES), lambda i, j: (i, 0, j, 0))
    out_shape = (
        jax.ShapeDtypeStruct((b, s, A_W), BF16),
        jax.ShapeDtypeStruct((b, s, A_W), BF16),
        jax.ShapeDtypeStruct((b, MOBA_HEADS * MOBA_VROWS, s), BF16),
        jax.ShapeDtypeStruct((b, C_W // V7X_LANES, s, V7X_LANES), F32),
        jax.ShapeDtypeStruct((b, C_W // V7X_LANES, s, V7X_LANES), F32),
        jax.ShapeDtypeStruct((b, C_W // V7X_LANES, s, V7X_LANES), F32),
        jax.ShapeDtypeStruct((b, s, 4 * A_W), BF16),
        jax.ShapeDtypeStruct((b, s, B_W), F32),
        jax.ShapeDtypeStruct((b, s, 2 * D_KW + D_W + D_KW), F32),
    )
    out_specs = (row3(A_W), row3(A_W), pl.BlockSpec((1, MOBA_HEADS * MOBA_VROWS, tm), lambda i, j: (i, 0, j)),
                 halves, halves, halves, row3(4 * A_W), row3(B_W), row3(2 * D_KW + D_W + D_KW))
    return pl.pallas_call(
        _inproj_kernel,
        out_shape=out_shape,
        grid=(b, s // tm),
        in_specs=[row3(d),
                  pl.BlockSpec((1, 1, mod_l.shape[-1]), lambda i, j: (i, 0, 0)),
                  per_layer(w_in_bf.shape),
                  row3(V7X_LANES), row3(V7X_LANES),
                  const2(gla_wr.shape), const2((1, D_KW))],
        out_specs=out_specs,
        compiler_params=_params("parallel", "parallel"),
        name="in_projection",
    )(x, mod_l.reshape(b, 1, -1), w_in_bf, cos_t, sin_t, gla_wr, gla_br.reshape(1, D_KW))


def _moba_kernel(q_ref, k_ref, vt_ref, o_ref, kmean_ref, qm_ref, unsel_ref, s_ref, p_ref, alpha_ref, acc_ref):
    blk = MOBA_BLOCK
    nb = k_ref.shape[1] // blk
    n = pl.program_id(1)

    @pl.when(n == 0)
    def _():
        for j in range(nb):
            kj = k_ref[0, j * blk:(j + 1) * blk, :].astype(F32)
            kmean_ref[j:j + 1, :] = jnp.mean(kj, axis=0, keepdims=True)

    q = q_ref[0]
    lane_head = lax.broadcasted_iota(jnp.int32, (1, A_W), 1) // HEAD_DIM
    km = kmean_ref[...]
    km_hi = km.astype(BF16)
    km_lo = (km - km_hi.astype(F32)).astype(BF16)
    key_idx = lax.broadcasted_iota(jnp.int32, (blk, blk), 0)
    qry_idx = lax.broadcasted_iota(jnp.int32, (blk, blk), 1)
    causal = key_idx <= qry_idx

    vr = MOBA_VROWS
    vth = lambda h, j: vt_ref[0, h * vr:(h + 1) * vr, pl.ds(pl.multiple_of(j * blk, blk), blk)]
    kblk = lambda j: k_ref[0, pl.ds(pl.multiple_of(j * blk, blk), blk), :]

    for h in range(MOBA_HEADS):
        qm_ref[h * blk:(h + 1) * blk, :] = jnp.where(lane_head == h, q, jnp.zeros_like(q))
    scores = lambda j: _dot_nt(kblk(j), qm_ref[...])

    blk_id = lax.broadcasted_iota(jnp.int32, (nb, MOBA_HEADS * blk), 0)
    own = _dot_nt(jnp.concatenate([kblk(n), km_hi, km_lo], axis=0), qm_ref[...])
    s_own = own[0:blk, :]
    gate = jnp.where(blk_id < n, own[blk:blk + nb, :] + own[blk + nb:blk + 2 * nb, :], NEG)
    s_ref[0] = scores(0)
    rank = jnp.zeros(gate.shape, jnp.int32)
    for i in range(nb):
        gi = gate[i:i + 1, :]
        tie = (blk_id > i).astype(jnp.int32)
        rank = rank + jnp.where(gi > gate, 1, jnp.where(gi == gate, tie, 0))
    unsel_ref[...] = jnp.where((rank < MOBA_TOPK) & (blk_id < n), 0.0, 1.0)

    def accumulate(slot, j_blk):
        for h in range(MOBA_HEADS):
            cols = slice(h * blk, (h + 1) * blk)
            acc_ref[h * vr:(h + 1) * vr, :] = (alpha_ref[slot, :, cols] * acc_ref[h * vr:(h + 1) * vr, :]
                                              + jnp.dot(vth(h, j_blk), p_ref[slot, :, cols],
                                                        preferred_element_type=F32))

    acc_ref[...] = jnp.zeros_like(acc_ref)
    alpha_ref[1] = jnp.ones(alpha_ref.shape[1:], F32)
    s = jnp.where(jnp.concatenate([causal] * MOBA_HEADS, axis=1), s_own, NEG)
    m_init = jnp.max(s, axis=0, keepdims=True)
    p_ref[1] = jnp.exp2(s - m_init).astype(BF16)

    def step(j, j_prev, slot, m):
        s = s_ref[slot]
        accumulate(1 - slot, j_prev)
        s_ref[1 - slot] = scores(jnp.minimum(j + 1, nb - 1))
        unsel = unsel_ref[pl.ds(j, 1), :] > 0.5
        m_new = jnp.maximum(m, jnp.where(unsel, NEG, jnp.max(s, axis=0, keepdims=True)))
        alpha_ref[slot] = jnp.exp2(m - m_new)
        p_ref[slot] = jnp.exp2(s - jnp.where(unsel, -NEG, m_new)).astype(BF16)
        return m_new

    def body(i, m):
        m = step(2 * i, jnp.where(i == 0, n, 2 * i - 1), 0, m)
        return step(2 * i + 1, 2 * i, 1, m)

    pairs = (n + 1) // 2
    lax.fori_loop(0, pairs, body, m_init)
    accumulate(1, jnp.where(pairs == 0, n, 2 * pairs - 1))
    outs = []
    for h in range(MOBA_HEADS):
        a = acc_ref[h * vr:(h + 1) * vr, :]
        outs.append(a[0:HEAD_DIM, :] / a[HEAD_DIM:HEAD_DIM + 1, :])
    o_ref[0] = jnp.concatenate(outs, axis=0).T.astype(o_ref.dtype)


def _moba(aq, ak, avt):
    b, s, _ = aq.shape
    blk = MOBA_BLOCK
    return pl.pallas_call(
        _moba_kernel,
        out_shape=jax.ShapeDtypeStruct((b, s, A_W), BF16),
        grid=(b, s // blk),
        in_specs=[pl.BlockSpec((1, blk, A_W), lambda i, n: (i, n, 0)),
                  pl.BlockSpec((1, s, A_W), lambda i, n: (i, 0, 0)),
                  pl.BlockSpec((1, MOBA_HEADS * MOBA_VROWS, s), lambda i, n: (i, 0, 0))],
        out_specs=pl.BlockSpec((1, blk, A_W), lambda i, n: (i, n, 0)),
        scratch_shapes=[pltpu.VMEM((s // blk, A_W), F32),
                        pltpu.VMEM((MOBA_HEADS * blk, A_W), BF16),
                        pltpu.VMEM((s // blk, MOBA_HEADS * blk), F32),
                        pltpu.VMEM((2, blk, MOBA_HEADS * blk), F32),
                        pltpu.VMEM((2, blk, MOBA_HEADS * blk), BF16),
                        pltpu.VMEM((2, 1, MOBA_HEADS * blk), F32),
                        pltpu.VMEM((MOBA_HEADS * MOBA_VROWS, blk), F32)],
        compiler_params=_params("arbitrary", "arbitrary"),
        name="moba_attention",
    )(aq, ak, avt)


def _lru_kernel(x_ref, cw_ref, cb_ref, wg_ref, bg_ref, lam_ref, o_ref, xs_ref, h_ref):
    tl = x_ref.shape[1]
    halo = V7X_SUBLANES
    i = pl.program_id(1)

    @pl.when(i == 0)
    def _():
        xs_ref[0:halo, :] = jnp.zeros((halo, B_W), F32)
        h_ref[...] = jnp.zeros_like(h_ref)

    x = x_ref[0]
    xs_ref[halo:, :] = x
    y = cb_ref[...] + cw_ref[CONV_WIDTH - 1:CONV_WIDTH, :] * x
    for k in range(CONV_WIDTH - 1):
        y = y + cw_ref[k:k + 1, :] * xs_ref[pl.ds(halo - (CONV_WIDTH - 1) + k, tl), :]
    xs_ref[0:halo, :] = x[tl - halo:, :]

    gates = jnp.dot(y.astype(BF16), wg_ref[...], preferred_element_type=F32) + bg_ref[...]
    r = _sigmoid(gates[:, 0:B_W])
    ig = _sigmoid(gates[:, B_W:])
    lam = lam_ref[...]
    softplus_neg_lam = jnp.maximum(-lam, 0.0) + jnp.log1p(jnp.exp(-jnp.abs(lam)))
    log_a = (-LRU_C) * r * softplus_neg_lam
    a = jnp.exp(log_a)
    u = jnp.sqrt(-jnp.tanh(log_a) * (a * a + 1.0)) * (ig * y)

    row = lax.broadcasted_iota(jnp.int32, (tl, 1), 0)
    step = 1
    while step < tl:
        if step % V7X_SUBLANES:
            keep = row >= step
            a_prev = jnp.where(keep, pltpu.roll(a, step, 0), 1.0)
            u_prev = jnp.where(keep, pltpu.roll(u, step, 0), 0.0)
            u = u + a * u_prev
            a = a * a_prev
        else:
            u = jnp.concatenate([u[:step], u[step:] + a[step:] * u[:tl - step]], axis=0)
            a = jnp.concatenate([a[:step], a[step:] * a[:tl - step]], axis=0)
        step *= 2
    h = u + a * h_ref[...]
    h_ref[...] = h[tl - 1:tl, :]
    o_ref[0] = h.astype(o_ref.dtype)


def _rg_lru(zb, conv_w, conv_b, w_gates, b_gates, lam):
    b, s, w = zb.shape
    tl = LRU_TL
    const = lambda shp: pl.BlockSpec(shp, lambda i, j: (0, 0))
    return pl.pallas_call(
        _lru_kernel,
        out_shape=jax.ShapeDtypeStruct((b, s, w), BF16),
        grid=(b, s // tl),
        in_specs=[pl.BlockSpec((1, tl, w), lambda i, j: (i, j, 0)),
                  const(conv_w.shape), const((1, w)), const(w_gates.shape), const((1, 2 * w)), const((1, w))],
        out_specs=pl.BlockSpec((1, tl, w), lambda i, j: (i, j, 0)),
        scratch_shapes=[pltpu.VMEM((tl + V7X_SUBLANES, w), F32), pltpu.VMEM((1, w), F32)],
        compiler_params=_params("arbitrary", "arbitrary"),
        name="rg_lru",
    )(zb, conv_w, conv_b.reshape(1, w), w_gates, b_gates.reshape(1, 2 * w), lam.reshape(1, w))


def _dilated_kernel(q_ref, k_ref, v_ref, o_ref, oacc_ref, lacc_ref):
    qb = DIL_QBLOCK
    tile = o_ref.shape[1]
    t0 = pl.program_id(1) * tile
    halves = range(C_W // V7X_LANES)

    def load(ref, rows):
        return jnp.concatenate([ref.at[half][rows, :] for half in halves], axis=-1)

    def store(ref, rows, value):
        for half in halves:
            ref.at[half][rows, :] = value[:, half * V7X_LANES:(half + 1) * V7X_LANES]

    lane_head = lax.broadcasted_iota(jnp.int32, (1, C_W), 1) // HEAD_DIM
    qi = lax.broadcasted_iota(jnp.int32, (qb, 2 * qb), 0)
    kj = lax.broadcasted_iota(jnp.int32, (qb, 2 * qb), 1)

    for ci, (window, d) in enumerate(DIL_CONFIGS):
        n_steps = window // d
        per_class = tile // (qb * d)
        first, last = ci == 0, ci == len(DIL_CONFIGS) - 1

        def units(i, carry, d=d, n_steps=n_steps, per_class=per_class, first=first, last=last):
            group = []
            for k in range(DIL_UNROLL):
                u = i * DIL_UNROLL + k
                r = u // per_class
                base = t0 // d + (u % per_class) * qb
                start = jnp.maximum(base - qb, 0)
                if d == 1:
                    tokens = lambda m0, count, r=r: pl.ds(pl.multiple_of(m0, qb), count)
                else:
                    tokens = lambda m0, count, r=r: pl.ds(m0 * d + r, count, stride=d)
                q = load(q_ref, tokens(base, qb))
                kk = load(k_ref, tokens(start, 2 * qb)).astype(BF16)
                vv = load(v_ref, tokens(start, 2 * qb)).astype(BF16)
                dist = (base - start) + qi - kj
                valid = (dist >= 0) & (dist <= n_steps)
                q_bf = q.astype(BF16)
                q_stack = jnp.concatenate([jnp.where(lane_head == h, q_bf, jnp.zeros_like(q_bf))
                                           for h in range(DIL_HEADS)], axis=0)
                s = _dot_nt(q_stack, kk)
                group.append((s, valid, vv, tokens(base - t0 // d, qb)))
            soft = []
            for s, valid, vv, local in group:
                s = jnp.where(jnp.concatenate([valid] * DIL_HEADS, axis=0), s, NEG)
                m = jnp.max(s, axis=-1, keepdims=True)
                e = jnp.exp2(s - m)
                soft.append((e.astype(BF16), m, jnp.sum(e, axis=-1, keepdims=True)))
            pvs = [jnp.dot(e, vv, preferred_element_type=F32)
                   for (e, _, _), (_, _, vv, _) in zip(soft, group)]
            for pv, (_, m, l), (_, _, _, local) in zip(pvs, soft, group):
                o_all = pv / l
                lse_all = m + jnp.log2(l)
                out = o_all[0:qb, :]
                lse = jnp.broadcast_to(lse_all[0:qb, :], (qb, C_W))
                for h in range(1, DIL_HEADS):
                    out = jnp.where(lane_head == h, o_all[h * qb:(h + 1) * qb, :], out)
                    lse = jnp.where(lane_head == h, lse_all[h * qb:(h + 1) * qb, :], lse)
                if not first:
                    o_prev = load(oacc_ref, local)
                    l_prev = load(lacc_ref, local)
                    top = jnp.maximum(l_prev, lse)
                    w_prev = jnp.exp2(l_prev - top)
                    w_cur = jnp.exp2(lse - top)
                    out = (w_prev * o_prev + w_cur * out) / (w_prev + w_cur)
                    lse = top + jnp.log2(w_prev + w_cur)
                if last:
                    store(o_ref, local, out)
                else:
                    store(oacc_ref, local, out)
                    store(lacc_ref, local, lse)
            return carry

        lax.fori_loop(0, tile // (qb * DIL_UNROLL), units, 0)


def _dilated(cq, ck, cv):
    b, nh, s, w = cq.shape
    tile = DIL_TILE
    whole = pl.BlockSpec((None, nh, s, w), lambda i, j: (i, 0, 0, 0))
    return pl.pallas_call(
        _dilated_kernel,
        out_shape=jax.ShapeDtypeStruct((b, nh, s, w), F32),
        grid=(b, s // tile),
        in_specs=[whole, whole, whole],
        out_specs=pl.BlockSpec((None, nh, tile, w), lambda i, j: (i, 0, j, 0)),
        scratch_shapes=[pltpu.VMEM((nh, tile, w), F32), pltpu.VMEM((nh, tile, w), F32)],
        compiler_params=_params("parallel", "arbitrary"),
        name="dilated_attention",
    )(cq, ck, cv)


def _gla_kernel(z_ref, gn_ref, o_ref, state_ref, b_ref, q_ref, k_ref, v_ref, t_ref, y_ref):
    c = GLA_CHUNK
    tg = z_ref.shape[1]
    nch = tg // c
    i = pl.program_id(1)

    @pl.when(i == 0)
    def _():
        state_ref[...] = jnp.zeros_like(state_ref)

    qs = z_ref[0, :, 0:D_KW]
    ks = z_ref[0, :, D_KW:2 * D_KW]
    vs = z_ref[0, :, 2 * D_KW:2 * D_KW + D_W]
    la = z_ref[0, :, 2 * D_KW + D_W:]

    v_bf = vs.astype(BF16)
    ri = lax.broadcasted_iota(jnp.int32, (tg, tg), 0)
    ci = lax.broadcasted_iota(jnp.int32, (tg, tg), 1)

    def decays(c):
        chunk_causal = (ri // c == ci // c) & (ci <= ri)
        tri = jnp.where(chunk_causal, 1.0, 0.0).astype(BF16)
        b = sum(jnp.dot(tri, part, preferred_element_type=F32) for part in _split3(la))
        b_last = jnp.broadcast_to(b.reshape(tg // c, c, D_KW)[:, c - 1:c, :], (tg // c, c, D_KW)).reshape(tg, D_KW)
        return chunk_causal, b, b_last

    def through_state(b, b_last, c):
        same_head_t = (lax.broadcasted_iota(jnp.int32, (D_W, D_KW), 0) // GLA_DV
                       == lax.broadcasted_iota(jnp.int32, (D_W, D_KW), 1) // GLA_DK)
        q_in_bf = (qs * jnp.exp(b)).astype(BF16)
        k_dec_bf = (ks * jnp.exp(b_last - b)).astype(BF16)
        st = state_ref[...]
        inter = []
        for ch in range(tg // c):
            r0 = ch * c
            inter.append(_dot_nt(q_in_bf[r0:r0 + c, :], st.astype(BF16)))
            kvt = lax.dot_general(v_bf[r0:r0 + c, :], k_dec_bf[r0:r0 + c, :], (((0,), (0,)), ((), ())),
                                  preferred_element_type=F32)
            st = st * jnp.exp(b_last[r0:r0 + 1, :]) + jnp.where(same_head_t, kvt, 0.0)
        state_ref[...] = st
        y_ref[...] = y_ref[...] + jnp.concatenate(inter, axis=0)

    safe = jnp.min(la) * GLA_FAST_CHUNK > -GLA_SAFE_DECAY

    @pl.when(safe)
    def _():
        chunk_causal, b, b_last = decays(GLA_FAST_CHUNK)
        q_in = qs * jnp.exp(b)
        k_grow = (ks * jnp.exp(-b)).astype(BF16)
        dk_head = lax.broadcasted_iota(jnp.int32, (1, D_KW), 1) // GLA_DK
        q_stack = jnp.concatenate([jnp.where(dk_head == h, q_in, 0.0) for h in range(GLA_HEADS)], axis=0)
        a = _dot_nt(q_stack.astype(BF16), k_grow)
        a = jnp.where(jnp.concatenate([chunk_causal] * GLA_HEADS, axis=0), a, 0.0)
        av = jnp.dot(a.astype(BF16), v_bf, preferred_element_type=F32)
        dv_head = lax.broadcasted_iota(jnp.int32, (1, D_W), 1) // GLA_DV
        y = av[0:tg, :]
        for h in range(1, GLA_HEADS):
            y = jnp.where(dv_head == h, av[h * tg:(h + 1) * tg, :], y)
        y_ref[...] = y
        through_state(b, b_last, GLA_FAST_CHUNK)

    @pl.when(jnp.logical_not(safe))
    def _():
        _, b, b_last = decays(c)
        b_ref[...] = b
        q_ref[...] = qs
        k_ref[...] = ks
        v_ref[...] = vs
        head_sum = (lax.broadcasted_iota(jnp.int32, (D_KW, D_W), 0) // GLA_DK
                    == lax.broadcasted_iota(jnp.int32, (D_KW, D_W), 1) // GLA_DV).astype(BF16)
        jrow = lax.broadcasted_iota(jnp.int32, (c, D_KW), 0)

        def chunk(ch, carry):
            r0 = pl.multiple_of(ch * c, c)
            bch = b_ref[pl.ds(r0, c), :]
            kch = k_ref[pl.ds(r0, c), :]

            def row_terms(ii, carry2):
                bi = b_ref[pl.ds(r0 + ii, 1), :]
                qi = q_ref[pl.ds(r0 + ii, 1), :]
                t = jnp.exp(jnp.minimum(bi - bch, 0.0)) * kch * qi
                t_ref[pl.ds(pl.multiple_of(ii * c, c), c), :] = jnp.where(jrow <= ii, t, 0.0)
                return carry2

            lax.fori_loop(0, c, row_terms, 0)
            a_b = jnp.dot(t_ref[...].astype(BF16), head_sum, preferred_element_type=F32)
            y_ref[pl.ds(r0, c), :] = jnp.sum(a_b.reshape(c, c, D_W) * v_ref[pl.ds(r0, c), :][None, :, :], axis=1)
            return carry

        lax.fori_loop(0, nch, chunk, 0)
        through_state(b, b_last, c)

    y = y_ref[...]

    e_head = jnp.where(lax.broadcasted_iota(jnp.int32, (D_W, D_W), 0) // GLA_DV
                       == lax.broadcasted_iota(jnp.int32, (D_W, D_W), 1) // GLA_DV, 1.0 / GLA_DV, 0.0).astype(BF16)
    yy = y * y
    yy_hi = yy.astype(BF16)
    yy_lo = (yy - yy_hi.astype(F32)).astype(BF16)
    ms = (jnp.dot(yy_hi, e_head, preferred_element_type=F32) + jnp.dot(yy_lo, e_head, preferred_element_type=F32))
    o_ref[0] = (y * lax.rsqrt(ms + LN_EPS) * gn_ref[...]).astype(o_ref.dtype)


def _gla(zd, gla_gn):
    b, s, w = zd.shape
    tg = GLA_TG
    gn_row = jnp.tile(gla_gn.astype(F32), GLA_HEADS).reshape(1, D_W)
    return pl.pallas_call(
        _gla_kernel,
        out_shape=jax.ShapeDtypeStruct((b, s, D_W), BF16),
        grid=(b, s // tg),
        in_specs=[pl.BlockSpec((1, tg, w), lambda i, j: (i, j, 0)),
                  pl.BlockSpec((1, D_W), lambda i, j: (0, 0))],
        out_specs=pl.BlockSpec((1, tg, D_W), lambda i, j: (i, j, 0)),
        scratch_shapes=[pltpu.VMEM((D_W, D_KW), F32),
                        pltpu.VMEM((tg, D_KW), F32),
                        pltpu.VMEM((tg, D_KW), F32),
                        pltpu.VMEM((tg, D_KW), F32),
                        pltpu.VMEM((tg, D_W), F32),
                        pltpu.VMEM((GLA_CHUNK * GLA_CHUNK, D_KW), F32),
                        pltpu.VMEM((tg, D_W), F32)],
        compiler_params=_params("arbitrary", "arbitrary"),
        name="gla",
    )(zd, gn_row)


def _outproj_kernel(ya_ref, yb_ref, yc_ref, yd_ref, zg_ref, x_ref, mod_ref, w_ref, g_ref, b_ref, o_ref, *, alpha):
    d = x_ref.shape[2]
    y = jnp.zeros((x_ref.shape[1], d), F32)
    yc = jnp.concatenate([yc_ref[0, half] for half in range(yc_ref.shape[1])], axis=-1)
    for j, yj in enumerate((ya_ref[0], yb_ref[0], yc, yd_ref[0])):
        mix = (yj.astype(F32) * zg_ref[0, :, j * A_W:(j + 1) * A_W].astype(F32)).astype(BF16)
        y = y + jnp.dot(mix, w_ref[j * A_W:(j + 1) * A_W, :], preferred_element_type=F32)
    gate = mod_ref[0, :, 2 * d:3 * d]
    r = alpha * x_ref[0] + (1.0 + gate) * y
    mu = jnp.mean(r, axis=-1, keepdims=True)
    rc = r - mu
    var = jnp.mean(rc * rc, axis=-1, keepdims=True)
    o_ref[0] = rc * lax.rsqrt(var + LN_EPS) * g_ref[...] + b_ref[...]


def _out_projection(ya, yb, yc, yd, zg, x, mod_l, w_out_bf, layer, ln_g, ln_b, alpha):
    b, s, d = x.shape
    tm = OUT_TM
    row3 = lambda w: pl.BlockSpec((1, tm, w), lambda i, j: (i, j, 0))
    const2 = lambda shp: pl.BlockSpec(shp, lambda i, j: (0, 0))
    return pl.pallas_call(
        functools.partial(_outproj_kernel, alpha=alpha),
        out_shape=jax.ShapeDtypeStruct((b, s, d), F32),
        grid=(b, s // tm),
        in_specs=[row3(A_W), row3(B_W),
                  pl.BlockSpec((1, yc.shape[1], tm, yc.shape[3]), lambda i, j: (i, 0, j, 0)),
                  row3(D_W), row3(4 * A_W), row3(d),
                  pl.BlockSpec((1, 1, mod_l.shape[-1]), lambda i, j: (i, 0, 0)),
                  pl.BlockSpec((None,) + w_out_bf.shape[1:], lambda i, j: (layer, 0, 0)),
                  const2((1, d)), const2((1, d))],
        out_specs=row3(d),
        compiler_params=_params("parallel", "parallel"),
        name="out_projection",
    )(ya, yb, yc, yd, zg, x, mod_l.reshape(b, 1, -1), w_out_bf, ln_g.reshape(1, d), ln_b.reshape(1, d))


def _block_diag(w):
    g, bd, _ = w.shape
    eye = jnp.eye(g, dtype=w.dtype)
    return (eye[:, None, :, None] * w[:, :, None, :]).reshape(g * bd, g * bd)


def kernel(x, c, positions, w_mod, b_mod, w_in, conv_w, conv_b, lru_wa, lru_ba, lru_wx, lru_bx, lru_lam,
           gla_wr, gla_br, gla_gn, w_out, ln_g, ln_b):
    depth = w_mod.shape[0]
    alpha = (2 * depth) ** 0.25
    mod = _modulation(c, w_mod, b_mod)
    cos_t, sin_t = _rope_tables(positions)
    w_in_bf = w_in.astype(BF16)
    w_out_bf = w_out.astype(BF16)
    for l in range(depth):
        aq, ak, avt, cq, ck, cv, zg, zb, zd = _in_projection(x, mod[l], w_in_bf, l, cos_t, sin_t,
                                                               gla_wr[l], gla_br[l])
        ya = _moba(aq, ak, avt)
        w_gates = jnp.concatenate([_block_diag(lru_wa[l]), _block_diag(lru_wx[l])], axis=1).astype(BF16)
        b_gates = jnp.concatenate([lru_ba[l], lru_bx[l]])
        yb = _rg_lru(zb, conv_w[l], conv_b[l], w_gates, b_gates, lru_lam[l])
        yc = _dilated(cq, ck, cv)
        yd = _gla(zd, gla_gn[l])
        x = _out_projection(ya, yb, yc, yd, zg, x, mod[l], w_out_bf, l, ln_g[l], ln_b[l], alpha)
    return x
```

```python
import functools

import jax
import jax.numpy as jnp
from jax import lax
from jax.experimental import pallas as pl
from jax.experimental.pallas import tpu as pltpu

F32 = jnp.float32
BF16 = jnp.bfloat16

HEAD_DIM = 64
MOBA_HEADS = 4
MOBA_BLOCK = 256
MOBA_TOPK = 3
LRU_WIDTH = 256
LRU_BLOCKS = 4
CONV_WIDTH = 4
LRU_C = 8.0
DIL_HEADS = 4
DIL_CONFIGS = ((128, 1), (512, 4), (2048, 16))
DIL_QBLOCK = 128
GLA_HEADS = 4
GLA_DK = 32
GLA_DV = 64
GLA_LOWRANK = 16
GLA_TAU = 16.0
GLA_CHUNK = 32
ROPE_THETA = 10000.0
LN_EPS = 1e-5
NEG = -1e30
LOG2E = 1.4426950408889634
MOBA_VROWS = HEAD_DIM + 16
GLA_SAFE_DECAY = 60.0
GLA_FAST_CHUNK = 128

A_W = MOBA_HEADS * HEAD_DIM
B_W = LRU_WIDTH
C_W = DIL_HEADS * HEAD_DIM
D_KW = GLA_HEADS * GLA_DK
D_W = GLA_HEADS * GLA_DV
OFF_AQ, OFF_AK, OFF_AV, OFF_AG = 0, A_W, 2 * A_W, 3 * A_W
OFF_BX = 4 * A_W
OFF_BG = OFF_BX + B_W
OFF_CQ = OFF_BG + B_W
OFF_CK, OFF_CV, OFF_CG = OFF_CQ + C_W, OFF_CQ + 2 * C_W, OFF_CQ + 3 * C_W
OFF_DQ = OFF_CQ + 4 * C_W
OFF_DK = OFF_DQ + D_KW
OFF_DV = OFF_DK + D_KW
OFF_DG = OFF_DV + D_W
OFF_DR = OFF_DG + D_W
D_IN = OFF_DR + GLA_LOWRANK

V7X_LANES = 128
V7X_SUBLANES = 8
V7X_VMEM_LIMIT_BYTES = 48 * 1024 * 1024

IN_TM = 512
IN_SPLIT = 2
OUT_TM = 1024
GLA_TG = 256
ROPE_TR = 512
ROPE_PACK = V7X_LANES // (HEAD_DIM // 2)
DIL_UNROLL = 2
DIL_TILE = DIL_QBLOCK * max(d for _, d in DIL_CONFIGS)


def _params(*sem):
    return pltpu.CompilerParams(dimension_semantics=sem, vmem_limit_bytes=V7X_VMEM_LIMIT_BYTES)


def _sigmoid(x):
    return 1.0 / (1.0 + jnp.exp(-x))


def _log_sigmoid(x):
    return jnp.minimum(x, 0.0) - jnp.log1p(jnp.exp(-jnp.abs(x)))


def _dot_nt(a, b, **kw):
    return lax.dot_general(a, b, (((1,), (1,)), ((), ())), preferred_element_type=F32, **kw)


def _mod_kernel(c_ref, w_ref, b_ref, o_ref):
    o_ref[0] = jnp.dot(c_ref[...].astype(BF16), w_ref[0].astype(BF16), preferred_element_type=F32) + b_ref[0]


def _modulation(c, w_mod, b_mod):
    depth, d, d3 = w_mod.shape
    b = c.shape[0]
    nt = d3 // d
    return pl.pallas_call(
        _mod_kernel,
        out_shape=jax.ShapeDtypeStruct((depth, b, d3), F32),
        grid=(depth, nt),
        in_specs=[pl.BlockSpec((b, d), lambda l, j: (0, 0)),
                  pl.BlockSpec((1, d, d), lambda l, j: (l, 0, j)),
                  pl.BlockSpec((1, 1, d), lambda l, j: (l, 0, j))],
        out_specs=pl.BlockSpec((1, b, d), lambda l, j: (l, 0, j)),
        compiler_params=_params("parallel", "parallel"),
        name="adaln_modulation",
    )(c, w_mod, b_mod.reshape(depth, 1, d3))


def _split3(x):
    hi = x.astype(BF16)
    mid = (x - hi.astype(F32)).astype(BF16)
    lo = (x - hi.astype(F32) - mid.astype(F32)).astype(BF16)
    return hi, mid, lo


def _rope_table_kernel(pos_ref, inv_ref, cos_ref, sin_ref):
    half = HEAD_DIM // 2
    rows = pos_ref.shape[1]
    ang = pos_ref[0].astype(F32) * inv_ref[...]
    src = lax.broadcasted_iota(jnp.int32, (V7X_LANES, V7X_LANES), 0)
    dst = lax.broadcasted_iota(jnp.int32, (V7X_LANES, V7X_LANES), 1)
    sign = jnp.where((dst % HEAD_DIM) < half, -1.0, 1.0)
    for table_ref, vals, signed in ((cos_ref, jnp.cos(ang), False), (sin_ref, jnp.sin(ang), True)):
        parts = _split3(vals)
        for r in range(ROPE_PACK):
            pick = src == r * half + dst % half
            spread = jnp.where(pick, sign if signed else 1.0, 0.0).astype(BF16)
            full = sum(jnp.dot(p, spread, preferred_element_type=F32) for p in parts)
            table_ref.at[0][pl.ds(r, rows, stride=ROPE_PACK), :] = full


def _rope_tables(positions):
    b, s = positions.shape
    half = HEAD_DIM // 2
    inv = ROPE_THETA ** (-jnp.arange(half, dtype=F32) / half)
    inv_row = jnp.tile(inv, ROPE_PACK).reshape(1, V7X_LANES)
    packed = jnp.repeat(positions.reshape(b, s // ROPE_PACK, ROPE_PACK), half, axis=-1)
    tab = jax.ShapeDtypeStruct((b, s, V7X_LANES), F32)
    out = pl.BlockSpec((1, ROPE_TR, V7X_LANES), lambda i, j: (i, j, 0))
    return pl.pallas_call(
        _rope_table_kernel,
        out_shape=(tab, tab),
        grid=(b, s // ROPE_TR),
        in_specs=[pl.BlockSpec((1, ROPE_TR // ROPE_PACK, V7X_LANES), lambda i, j: (i, j, 0)),
                  pl.BlockSpec((1, V7X_LANES), lambda i, j: (0, 0))],
        out_specs=(out, out),
        compiler_params=_params("parallel", "parallel"),
        name="rope_tables",
    )(packed, inv_row)


def _rope(z, cos, sin):
    lane = lax.broadcasted_iota(jnp.int32, (1, V7X_LANES), 1)
    first_half = (lane % HEAD_DIM) < (HEAD_DIM // 2)
    outs = []
    for s in range(z.shape[1] // V7X_LANES):
        zs = z[:, s * V7X_LANES:(s + 1) * V7X_LANES]
        swapped = jnp.where(first_half, pltpu.roll(zs, V7X_LANES - HEAD_DIM // 2, 1), pltpu.roll(zs, HEAD_DIM // 2, 1))
        outs.append(zs * cos + swapped * sin)
    return jnp.concatenate(outs, axis=-1)


def _inproj_kernel(x_ref, mod_ref, w_ref, cos_ref, sin_ref, wr_ref, br_ref,
                   aq_ref, ak_ref, avt_ref, cq_ref, ck_ref, cv_ref, zg_ref, zb_ref, zd_ref):
    d = x_ref.shape[2]
    shift = mod_ref[0, :, 0:d]
    scale = mod_ref[0, :, d:2 * d]
    q_scale = HEAD_DIM ** -0.5
    part_rows = x_ref.shape[1] // IN_SPLIT
    for part in range(IN_SPLIT):
        rows = slice(part * part_rows, (part + 1) * part_rows)
        x = x_ref[0, rows, :]
        mu = jnp.mean(x, axis=-1, keepdims=True)
        xc = x - mu
        var = jnp.mean(xc * xc, axis=-1, keepdims=True)
        xn = xc * lax.rsqrt(var + LN_EPS)
        u = (xn * (1.0 + scale) + shift).astype(BF16)

        def proj(off, width, u=u):
            return jnp.dot(u, w_ref[:, off:off + width], preferred_element_type=F32)

        cos = cos_ref[0, rows, :]
        sin = sin_ref[0, rows, :]
        aq_ref[0, rows, :] = (_rope(proj(OFF_AQ, A_W), cos, sin) * (q_scale * LOG2E)).astype(BF16)
        ak_ref[0, rows, :] = _rope(proj(OFF_AK, A_W), cos, sin).astype(BF16)
        vt = proj(OFF_AV, A_W).T.astype(BF16)
        ones = jnp.ones((MOBA_VROWS - HEAD_DIM, part_rows), BF16)
        for h in range(MOBA_HEADS):
            avt_ref[0, h * MOBA_VROWS:h * MOBA_VROWS + HEAD_DIM, rows] = vt[h * HEAD_DIM:(h + 1) * HEAD_DIM, :]
            avt_ref[0, h * MOBA_VROWS + HEAD_DIM:(h + 1) * MOBA_VROWS, rows] = ones
        for ref, z in ((cq_ref, _rope(proj(OFF_CQ, C_W), cos, sin) * (q_scale * LOG2E)),
                       (ck_ref, _rope(proj(OFF_CK, C_W), cos, sin)),
                       (cv_ref, proj(OFF_CV, C_W))):
            for half in range(C_W // V7X_LANES):
                ref[0, half, rows, :] = z[:, half * V7X_LANES:(half + 1) * V7X_LANES]
        for j, off in enumerate((OFF_AG, OFF_BG, OFF_CG, OFF_DG)):
            g = proj(off, A_W)
            zg_ref[0, rows, j * A_W:(j + 1) * A_W] = (g * _sigmoid(g)).astype(BF16)
        zb_ref[0, rows, :] = proj(OFF_BX, B_W)
        dqk = proj(OFF_DQ, 2 * D_KW)
        zd_ref[0, rows, 0:D_KW] = dqk[:, 0:D_KW] * (GLA_DK ** -0.5)
        zd_ref[0, rows, D_KW:2 * D_KW] = dqk[:, D_KW:]
        zd_ref[0, rows, 2 * D_KW:2 * D_KW + D_W] = proj(OFF_DV, D_W)
        dr = proj(OFF_DR, GLA_LOWRANK)
        la = jnp.dot(dr.astype(BF16), wr_ref[...].astype(BF16), preferred_element_type=F32) + br_ref[...]
        zd_ref[0, rows, 2 * D_KW + D_W:] = _log_sigmoid(la) * (1.0 / GLA_TAU)


def _in_projection(x, mod_l, w_in_bf, layer, cos_t, sin_t, gla_wr, gla_br):
    b, s, d = x.shape
    tm = IN_TM
    row3 = lambda w: pl.BlockSpec((1, tm, w), lambda i, j: (i, j, 0))
    const2 = lambda shp: pl.BlockSpec(shp, lambda i, j: (0, 0))
    per_layer = lambda shp: pl.BlockSpec((None,) + shp[1:], lambda i, j: (layer, 0, 0))
    halves = pl.BlockSpec((1, C_W // V7X_LANES, tm, V7X_LANES), lambda i, j: (i, 0, j, 0))
    out_shape = (
        jax.ShapeDtypeStruct((b, s, A_W), BF16),
        jax.ShapeDtypeStruct((b, s, A_W), BF16),
        jax.ShapeDtypeStruct((b, MOBA_HEADS * MOBA_VROWS, s), BF16),
        jax.ShapeDtypeStruct((b, C_W // V7X_LANES, s, V7X_LANES), F32),
        jax.ShapeDtypeStruct((b, C_W // V7X_LANES, s, V7X_LANES), F32),
        jax.ShapeDtypeStruct((b, C_W // V7X_LANES, s, V7X_LANES), F32),
        jax.ShapeDtypeStruct((b, s, 4 * A_W), BF16),
        jax.ShapeDtypeStruct((b, s, B_W), F32),
        jax.ShapeDtypeStruct((b, s, 2 * D_KW + D_W + D_KW), F32),
    )
    out_specs = (row3(A_W), row3(A_W), pl.BlockSpec((1, MOBA_HEADS * MOBA_VROWS, tm), lambda i, j: (i, 0, j)),
                 halves, halves, halves, row3(4 * A_W), row3(B_W), row3(2 * D_KW + D_W + D_KW))
    return pl.pallas_call(
        _inproj_kernel,
        out_shape=out_shape,
        grid=(b, s // tm),
        in_specs=[row3(d),
                  pl.BlockSpec((1, 1, mod_l.shape[-1]), lambda i, j: (i, 0, 0)),
                  per_layer(w_in_bf.shape),
                  row3(V7X_LANES), row3(V7X_LANES),
                  const2(gla_wr.shape), const2((1, D_KW))],
        out_specs=out_specs,
        compiler_params=_params("parallel", "parallel"),
        name="in_projection",
    )(x, mod_l.reshape(b, 1, -1), w_in_bf, cos_t, sin_t, gla_wr, gla_br.reshape(1, D_KW))


def _moba_kernel(q_ref, k_ref, vt_ref, o_ref, kmean_ref, qm_ref, unsel_ref, s_ref, p_ref, alpha_ref, acc_ref):
    blk = MOBA_BLOCK
    nb = k_ref.shape[1] // blk
    n = pl.program_id(1)

    @pl.when(n == 0)
    def _():
        for j in range(nb):
            kj = k_ref[0, j * blk:(j + 1) * blk, :].astype(F32)
            kmean_ref[j:j + 1, :] = jnp.mean(kj, axis=0, keepdims=True)

    q = q_ref[0]
    lane_head = lax.broadcasted_iota(jnp.int32, (1, A_W), 1) // HEAD_DIM
    km = kmean_ref[...]
    km_hi = km.astype(BF16)
    km_lo = (km - km_hi.astype(F32)).astype(BF16)
    key_idx = lax.broadcasted_iota(jnp.int32, (blk, blk), 0)
    qry_idx = lax.broadcasted_iota(jnp.int32, (blk, blk), 1)
    causal = key_idx <= qry_idx

    vr = MOBA_VROWS
    vth = lambda h, j: vt_ref[0, h * vr:(h + 1) * vr, pl.ds(pl.multiple_of(j * blk, blk), blk)]
    kblk = lambda j: k_ref[0, pl.ds(pl.multiple_of(j * blk, blk), blk), :]

    for h in range(MOBA_HEADS):
        qm_ref[h * blk:(h + 1) * blk, :] = jnp.where(lane_head == h, q, jnp.zeros_like(q))
    scores = lambda j: _dot_nt(kblk(j), qm_ref[...])

    blk_id = lax.broadcasted_iota(jnp.int32, (nb, MOBA_HEADS * blk), 0)
    own = _dot_nt(jnp.concatenate([kblk(n), km_hi, km_lo], axis=0), qm_ref[...])
    s_own = own[0:blk, :]
    gate = jnp.where(blk_id < n, own[blk:blk + nb, :] + own[blk + nb:blk + 2 * nb, :], NEG)
    s_ref[0] = scores(0)
    rank = jnp.zeros(gate.shape, jnp.int32)
    for i in range(nb):
        gi = gate[i:i + 1, :]
        tie = (blk_id > i).astype(jnp.int32)
        rank = rank + jnp.where(gi > gate, 1, jnp.where(gi == gate, tie, 0))
    unsel_ref[...] = jnp.where((rank < MOBA_TOPK) & (blk_id < n), 0.0, 1.0)

    def accumulate(slot, j_blk):
        for h in range(MOBA_HEADS):
            cols = slice(h * blk, (h + 1) * blk)
            acc_ref[h * vr:(h + 1) * vr, :] = (alpha_ref[slot, :, cols] * acc_ref[h * vr:(h + 1) * vr, :]
                                              + jnp.dot(vth(h, j_blk), p_ref[slot, :, cols],
                                                        preferred_element_type=F32))

    acc_ref[...] = jnp.zeros_like(acc_ref)
    alpha_ref[1] = jnp.ones(alpha_ref.shape[1:], F32)
    s = jnp.where(jnp.concatenate([causal] * MOBA_HEADS, axis=1), s_own, NEG)
    m_init = jnp.max(s, axis=0, keepdims=True)
    p_ref[1] = jnp.exp2(s - m_init).astype(BF16)

    def step(j, j_prev, slot, m, prefetch=True):
        s = s_ref[slot]
        accumulate(1 - slot, j_prev)
        if prefetch:
            s_ref[1 - slot] = scores(jnp.minimum(j + 1, nb - 1))
        unsel = unsel_ref[pl.ds(j, 1), :] > 0.5
        m_new = jnp.maximum(m, jnp.where(unsel, NEG, jnp.max(s, axis=0, keepdims=True)))
        alpha_ref[slot] = jnp.exp2(m - m_new)
        p_ref[slot] = jnp.exp2(s - jnp.where(unsel, -NEG, m_new)).astype(BF16)
        return m_new

    def body(i, m):
        m = step(2 * i, jnp.where(i == 0, n, 2 * i - 1), 0, m)
        return step(2 * i + 1, 2 * i, 1, m)

    m_pairs = lax.fori_loop(0, n // 2, body, m_init)

    @pl.when(n % 2 == 1)
    def _():
        step(n - 1, jnp.where(n == 1, n, n - 2), 0, m_pairs, prefetch=False)
        accumulate(0, n - 1)

    @pl.when(n % 2 == 0)
    def _():
        accumulate(1, jnp.where(n == 0, n, n - 1))
    outs = []
    for h in range(MOBA_HEADS):
        a = acc_ref[h * vr:(h + 1) * vr, :]
        outs.append(a[0:HEAD_DIM, :] / a[HEAD_DIM:HEAD_DIM + 1, :])
    o_ref[0] = jnp.concatenate(outs, axis=0).T.astype(o_ref.dtype)


def _moba(aq, ak, avt):
    b, s, _ = aq.shape
    blk = MOBA_BLOCK
    return pl.pallas_call(
        _moba_kernel,
        out_shape=jax.ShapeDtypeStruct((b, s, A_W), BF16),
        grid=(b, s // blk),
        in_specs=[pl.BlockSpec((1, blk, A_W), lambda i, n: (i, n, 0)),
                  pl.BlockSpec((1, s, A_W), lambda i, n: (i, 0, 0)),
                  pl.BlockSpec((1, MOBA_HEADS * MOBA_VROWS, s), lambda i, n: (i, 0, 0))],
        out_specs=pl.BlockSpec((1, blk, A_W), lambda i, n: (i, n, 0)),
        scratch_shapes=[pltpu.VMEM((s // blk, A_W), F32),
                        pltpu.VMEM((MOBA_HEADS * blk, A_W), BF16),
                        pltpu.VMEM((s // blk, MOBA_HEADS * blk), F32),
                        pltpu.VMEM((2, blk, MOBA_HEADS * blk), F32),
                        pltpu.VMEM((2, blk, MOBA_HEADS * blk), BF16),
                        pltpu.VMEM((2, 1, MOBA_HEADS * blk), F32),
                        pltpu.VMEM((MOBA_HEADS * MOBA_VROWS, blk), F32)],
        compiler_params=_params("arbitrary", "arbitrary"),
        name="moba_attention",
    )(aq, ak, avt)


def _lru_reset(xs_ref, h_ref):
    xs_ref[0:V7X_SUBLANES, :] = jnp.zeros((V7X_SUBLANES, B_W), F32)
    h_ref[...] = jnp.zeros_like(h_ref)


def _lru_tile(x_ref, cw_ref, cb_ref, wg_ref, bg_ref, lam_ref, o_ref, xs_ref, h_ref):
    tl = x_ref.shape[1]
    halo = V7X_SUBLANES
    x = x_ref[0]
    xs_ref[halo:, :] = x
    y = cb_ref[...] + cw_ref[CONV_WIDTH - 1:CONV_WIDTH, :] * x
    for k in range(CONV_WIDTH - 1):
        y = y + cw_ref[k:k + 1, :] * xs_ref[pl.ds(halo - (CONV_WIDTH - 1) + k, tl), :]
    xs_ref[0:halo, :] = x[tl - halo:, :]

    gates = jnp.dot(y.astype(BF16), wg_ref[...], preferred_element_type=F32) + bg_ref[...]
    r = _sigmoid(gates[:, 0:B_W])
    ig = _sigmoid(gates[:, B_W:])
    lam = lam_ref[...]
    softplus_neg_lam = jnp.maximum(-lam, 0.0) + jnp.log1p(jnp.exp(-jnp.abs(lam)))
    log_a = (-LRU_C) * r * softplus_neg_lam
    a = jnp.exp(log_a)
    u = jnp.sqrt(-jnp.tanh(log_a) * (a * a + 1.0)) * (ig * y)

    row = lax.broadcasted_iota(jnp.int32, (tl, 1), 0)
    step = 1
    while step < tl:
        if step % V7X_SUBLANES:
            keep = row >= step
            a_prev = jnp.where(keep, pltpu.roll(a, step, 0), 1.0)
            u_prev = jnp.where(keep, pltpu.roll(u, step, 0), 0.0)
            u = u + a * u_prev
            a = a * a_prev
        else:
            u = jnp.concatenate([u[:step], u[step:] + a[step:] * u[:tl - step]], axis=0)
            a = jnp.concatenate([a[:step], a[step:] * a[:tl - step]], axis=0)
        step *= 2
    h = u + a * h_ref[...]
    h_ref[...] = h[tl - 1:tl, :]
    o_ref[0] = h.astype(o_ref.dtype)


def _dilated_kernel(q_ref, k_ref, v_ref, o_ref, oacc_ref, lacc_ref):
    qb = DIL_QBLOCK
    tile = o_ref.shape[1]
    t0 = pl.program_id(1) * tile
    halves = range(C_W // V7X_LANES)

    def load(ref, rows):
        return jnp.concatenate([ref.at[half][rows, :] for half in halves], axis=-1)

    def store(ref, rows, value):
        for half in halves:
            ref.at[half][rows, :] = value[:, half * V7X_LANES:(half + 1) * V7X_LANES]

    lane_head = lax.broadcasted_iota(jnp.int32, (1, C_W), 1) // HEAD_DIM
    qi = lax.broadcasted_iota(jnp.int32, (qb, 2 * qb), 0)
    kj = lax.broadcasted_iota(jnp.int32, (qb, 2 * qb), 1)

    for ci, (window, d) in enumerate(DIL_CONFIGS):
        n_steps = window // d
        per_class = tile // (qb * d)
        first, last = ci == 0, ci == len(DIL_CONFIGS) - 1

        def units(i, carry, d=d, n_steps=n_steps, per_class=per_class, first=first, last=last):
            group = []
            for k in range(DIL_UNROLL):
                u = i * DIL_UNROLL + k
                r = u // per_class
                base = t0 // d + (u % per_class) * qb
                start = jnp.maximum(base - qb, 0)
                if d == 1:
                    tokens = lambda m0, count, r=r: pl.ds(pl.multiple_of(m0, qb), count)
                else:
                    tokens = lambda m0, count, r=r: pl.ds(m0 * d + r, count, stride=d)
                q = load(q_ref, tokens(base, qb))
                kk = load(k_ref, tokens(start, 2 * qb)).astype(BF16)
                vv = load(v_ref, tokens(start, 2 * qb)).astype(BF16)
                dist = (base - start) + qi - kj
                valid = (dist >= 0) & (dist <= n_steps)
                q_bf = q.astype(BF16)
                q_stack = jnp.concatenate([jnp.where(lane_head == h, q_bf, jnp.zeros_like(q_bf))
                                           for h in range(DIL_HEADS)], axis=0)
                s = _dot_nt(q_stack, kk)
                group.append((s, valid, vv, tokens(base - t0 // d, qb)))
            soft = []
            for s, valid, vv, local in group:
                s = jnp.where(jnp.concatenate([valid] * DIL_HEADS, axis=0), s, NEG)
                m = jnp.max(s, axis=-1, keepdims=True)
                e = jnp.exp2(s - m)
                soft.append((e.astype(BF16), m, jnp.sum(e, axis=-1, keepdims=True)))
            pvs = [jnp.dot(e, vv, preferred_element_type=F32)
                   for (e, _, _), (_, _, vv, _) in zip(soft, group)]
            for pv, (_, m, l), (_, _, _, local) in zip(pvs, soft, group):
                o_all = pv / l
                lse_all = m + jnp.log2(l)
                out = o_all[0:qb, :]
                lse = jnp.broadcast_to(lse_all[0:qb, :], (qb, C_W))
                for h in range(1, DIL_HEADS):
                    out = jnp.where(lane_head == h, o_all[h * qb:(h + 1) * qb, :], out)
                    lse = jnp.where(lane_head == h, lse_all[h * qb:(h + 1) * qb, :], lse)
                if not first:
                    o_prev = load(oacc_ref, local)
                    l_prev = load(lacc_ref, local)
                    top = jnp.maximum(l_prev, lse)
                    w_prev = jnp.exp2(l_prev - top)
                    w_cur = jnp.exp2(lse - top)
                    out = (w_prev * o_prev + w_cur * out) / (w_prev + w_cur)
                    lse = top + jnp.log2(w_prev + w_cur)
                if last:
                    store(o_ref, local, out)
                else:
                    store(oacc_ref, local, out)
                    store(lacc_ref, local, lse)
            return carry

        lax.fori_loop(0, tile // (qb * DIL_UNROLL), units, 0)


def _dilated(cq, ck, cv):
    b, nh, s, w = cq.shape
    tile = DIL_TILE
    whole = pl.BlockSpec((None, nh, s, w), lambda i, j: (i, 0, 0, 0))
    return pl.pallas_call(
        _dilated_kernel,
        out_shape=jax.ShapeDtypeStruct((b, nh, s, w), F32),
        grid=(b, s // tile),
        in_specs=[whole, whole, whole],
        out_specs=pl.BlockSpec((None, nh, tile, w), lambda i, j: (i, 0, j, 0)),
        scratch_shapes=[pltpu.VMEM((nh, tile, w), F32), pltpu.VMEM((nh, tile, w), F32)],
        compiler_params=_params("parallel", "arbitrary"),
        name="dilated_attention",
    )(cq, ck, cv)


def _gla_tile(z_ref, gn_ref, o_ref, state_ref, b_ref, q_ref, k_ref, v_ref, t_ref, y_ref, side_work):
    c = GLA_CHUNK
    tg = z_ref.shape[1]
    nch = tg // c
    i = pl.program_id(1)

    @pl.when(i == 0)
    def _():
        state_ref[...] = jnp.zeros_like(state_ref)

    qs = z_ref[0, :, 0:D_KW]
    ks = z_ref[0, :, D_KW:2 * D_KW]
    vs = z_ref[0, :, 2 * D_KW:2 * D_KW + D_W]
    la = z_ref[0, :, 2 * D_KW + D_W:]

    v_bf = vs.astype(BF16)
    ri = lax.broadcasted_iota(jnp.int32, (tg, tg), 0)
    ci = lax.broadcasted_iota(jnp.int32, (tg, tg), 1)

    def decays(c):
        chunk_causal = (ri // c == ci // c) & (ci <= ri)
        tri = jnp.where(chunk_causal, 1.0, 0.0).astype(BF16)
        b = sum(jnp.dot(tri, part, preferred_element_type=F32) for part in _split3(la))
        b_last = jnp.broadcast_to(b.reshape(tg // c, c, D_KW)[:, c - 1:c, :], (tg // c, c, D_KW)).reshape(tg, D_KW)
        return chunk_causal, b, b_last

    def through_state(b, b_last, c):
        same_head_t = (lax.broadcasted_iota(jnp.int32, (D_W, D_KW), 0) // GLA_DV
                       == lax.broadcasted_iota(jnp.int32, (D_W, D_KW), 1) // GLA_DK)
        q_in_bf = (qs * jnp.exp(b)).astype(BF16)
        k_dec_bf = (ks * jnp.exp(b_last - b)).astype(BF16)
        st = state_ref[...]
        inter = []
        for ch in range(tg // c):
            r0 = ch * c
            inter.append(_dot_nt(q_in_bf[r0:r0 + c, :], st.astype(BF16)))
            kvt = lax.dot_general(v_bf[r0:r0 + c, :], k_dec_bf[r0:r0 + c, :], (((0,), (0,)), ((), ())),
                                  preferred_element_type=F32)
            st = st * jnp.exp(b_last[r0:r0 + 1, :]) + jnp.where(same_head_t, kvt, 0.0)
        state_ref[...] = st
        y_ref[...] = y_ref[...] + jnp.concatenate(inter, axis=0)

    safe = jnp.min(la) * GLA_FAST_CHUNK > -GLA_SAFE_DECAY

    @pl.when(safe)
    def _():
        side_work()
        chunk_causal, b, b_last = decays(GLA_FAST_CHUNK)
        q_in = qs * jnp.exp(b)
        k_grow = (ks * jnp.exp(-b)).astype(BF16)
        dk_head = lax.broadcasted_iota(jnp.int32, (1, D_KW), 1) // GLA_DK
        q_stack = jnp.concatenate([jnp.where(dk_head == h, q_in, 0.0) for h in range(GLA_HEADS)], axis=0)
        a = _dot_nt(q_stack.astype(BF16), k_grow)
        a = jnp.where(jnp.concatenate([chunk_causal] * GLA_HEADS, axis=0), a, 0.0)
        av = jnp.dot(a.astype(BF16), v_bf, preferred_element_type=F32)
        dv_head = lax.broadcasted_iota(jnp.int32, (1, D_W), 1) // GLA_DV
        y = av[0:tg, :]
        for h in range(1, GLA_HEADS):
            y = jnp.where(dv_head == h, av[h * tg:(h + 1) * tg, :], y)
        y_ref[...] = y
        through_state(b, b_last, GLA_FAST_CHUNK)

    @pl.when(jnp.logical_not(safe))
    def _():
        side_work()
        _, b, b_last = decays(c)
        b_ref[...] = b
        q_ref[...] = qs
        k_ref[...] = ks
        v_ref[...] = vs
        head_sum = (lax.broadcasted_iota(jnp.int32, (D_KW, D_W), 0) // GLA_DK
                    == lax.broadcasted_iota(jnp.int32, (D_KW, D_W), 1) // GLA_DV).astype(BF16)
        jrow = lax.broadcasted_iota(jnp.int32, (c, D_KW), 0)

        def chunk(ch, carry):
            r0 = pl.multiple_of(ch * c, c)
            bch = b_ref[pl.ds(r0, c), :]
            kch = k_ref[pl.ds(r0, c), :]

            def row_terms(ii, carry2):
                bi = b_ref[pl.ds(r0 + ii, 1), :]
                qi = q_ref[pl.ds(r0 + ii, 1), :]
                t = jnp.exp(jnp.minimum(bi - bch, 0.0)) * kch * qi
                t_ref[pl.ds(pl.multiple_of(ii * c, c), c), :] = jnp.where(jrow <= ii, t, 0.0)
                return carry2

            lax.fori_loop(0, c, row_terms, 0)
            a_b = jnp.dot(t_ref[...].astype(BF16), head_sum, preferred_element_type=F32)
            y_ref[pl.ds(r0, c), :] = jnp.sum(a_b.reshape(c, c, D_W) * v_ref[pl.ds(r0, c), :][None, :, :], axis=1)
            return carry

        lax.fori_loop(0, nch, chunk, 0)
        through_state(b, b_last, c)

    y = y_ref[...]

    e_head = jnp.where(lax.broadcasted_iota(jnp.int32, (D_W, D_W), 0) // GLA_DV
                       == lax.broadcasted_iota(jnp.int32, (D_W, D_W), 1) // GLA_DV, 1.0 / GLA_DV, 0.0).astype(BF16)
    yy = y * y
    yy_hi = yy.astype(BF16)
    yy_lo = (yy - yy_hi.astype(F32)).astype(BF16)
    ms = (jnp.dot(yy_hi, e_head, preferred_element_type=F32) + jnp.dot(yy_lo, e_head, preferred_element_type=F32))
    o_ref[0] = (y * lax.rsqrt(ms + LN_EPS) * gn_ref[...]).astype(o_ref.dtype)


def _recurrent_kernel(zb_ref, cw_ref, cb_ref, wg_ref, bg_ref, lam_ref, zd_ref, gn_ref, yb_ref, yd_ref,
                      xs_ref, h_ref, state_ref, b_ref, q_ref, k_ref, v_ref, t_ref, y_ref):
    @pl.when(pl.program_id(1) == 0)
    def _():
        _lru_reset(xs_ref, h_ref)

    lru = functools.partial(_lru_tile, zb_ref, cw_ref, cb_ref, wg_ref, bg_ref, lam_ref, yb_ref, xs_ref, h_ref)
    _gla_tile(zd_ref, gn_ref, yd_ref, state_ref, b_ref, q_ref, k_ref, v_ref, t_ref, y_ref, lru)


def _recurrent_mixers(zb, conv_w, conv_b, w_gates, b_gates, lam, zd, gla_gn):
    b, s, w = zb.shape
    wd = zd.shape[2]
    tg = GLA_TG
    gn_row = jnp.tile(gla_gn.astype(F32), GLA_HEADS).reshape(1, D_W)
    const = lambda shp: pl.BlockSpec(shp, lambda i, j: (0, 0))
    rows = lambda width: pl.BlockSpec((1, tg, width), lambda i, j: (i, j, 0))
    return pl.pallas_call(
        _recurrent_kernel,
        out_shape=(jax.ShapeDtypeStruct((b, s, w), BF16), jax.ShapeDtypeStruct((b, s, D_W), BF16)),
        grid=(b, s // tg),
        in_specs=[rows(w), const(conv_w.shape), const((1, w)), const(w_gates.shape), const((1, 2 * w)),
                  const((1, w)), rows(wd), const((1, D_W))],
        out_specs=(rows(w), rows(D_W)),
        scratch_shapes=[pltpu.VMEM((tg + V7X_SUBLANES, w), F32),
                        pltpu.VMEM((1, w), F32),
                        pltpu.VMEM((D_W, D_KW), F32),
                        pltpu.VMEM((tg, D_KW), F32),
                        pltpu.VMEM((tg, D_KW), F32),
                        pltpu.VMEM((tg, D_KW), F32),
                        pltpu.VMEM((tg, D_W), F32),
                        pltpu.VMEM((GLA_CHUNK * GLA_CHUNK, D_KW), F32),
                        pltpu.VMEM((tg, D_W), F32)],
        compiler_params=_params("arbitrary", "arbitrary"),
        name="rg_lru_and_gla",
    )(zb, conv_w, conv_b.reshape(1, w), w_gates, b_gates.reshape(1, 2 * w), lam.reshape(1, w), zd, gn_row)


def _outproj_kernel(ya_ref, yb_ref, yc_ref, yd_ref, zg_ref, x_ref, mod_ref, w_ref, g_ref, b_ref, o_ref, *, alpha):
    d = x_ref.shape[2]
    y = jnp.zeros((x_ref.shape[1], d), F32)
    yc = jnp.concatenate([yc_ref[0, half] for half in range(yc_ref.shape[1])], axis=-1)
    for j, yj in enumerate((ya_ref[0], yb_ref[0], yc, yd_ref[0])):
        mix = (yj.astype(F32) * zg_ref[0, :, j * A_W:(j + 1) * A_W].astype(F32)).astype(BF16)
        y = y + jnp.dot(mix, w_ref[j * A_W:(j + 1) * A_W, :], preferred_element_type=F32)
    gate = mod_ref[0, :, 2 * d:3 * d]
    r = alpha * x_ref[0] + (1.0 + gate) * y
    mu = jnp.mean(r, axis=-1, keepdims=True)
    rc = r - mu
    var = jnp.mean(rc * rc, axis=-1, keepdims=True)
    o_ref[0] = rc * lax.rsqrt(var + LN_EPS) * g_ref[...] + b_ref[...]


def _out_projection(ya, yb, yc, yd, zg, x, mod_l, w_out_bf, layer, ln_g, ln_b, alpha):
    b, s, d = x.shape
    tm = OUT_TM
    row3 = lambda w: pl.BlockSpec((1, tm, w), lambda i, j: (i, j, 0))
    const2 = lambda shp: pl.BlockSpec(shp, lambda i, j: (0, 0))
    return pl.pallas_call(
        functools.partial(_outproj_kernel, alpha=alpha),
        out_shape=jax.ShapeDtypeStruct((b, s, d), F32),
        grid=(b, s // tm),
        in_specs=[row3(A_W), row3(B_W),
                  pl.BlockSpec((1, yc.shape[1], tm, yc.shape[3]), lambda i, j: (i, 0, j, 0)),
                  row3(D_W), row3(4 * A_W), row3(d),
                  pl.BlockSpec((1, 1, mod_l.shape[-1]), lambda i, j: (i, 0, 0)),
                  pl.BlockSpec((None,) + w_out_bf.shape[1:], lambda i, j: (layer, 0, 0)),
                  const2((1, d)), const2((1, d))],
        out_specs=row3(d),
        compiler_params=_params("parallel", "parallel"),
        name="out_projection",
    )(ya, yb, yc, yd, zg, x, mod_l.reshape(b, 1, -1), w_out_bf, ln_g.reshape(1, d), ln_b.reshape(1, d))


def _block_diag(w):
    g, bd, _ = w.shape
    eye = jnp.eye(g, dtype=w.dtype)
    return (eye[:, None, :, None] * w[:, :, None, :]).reshape(g * bd, g * bd)


def kernel(x, c, positions, w_mod, b_mod, w_in, conv_w, conv_b, lru_wa, lru_ba, lru_wx, lru_bx, lru_lam,
           gla_wr, gla_br, gla_gn, w_out, ln_g, ln_b):
    depth = w_mod.shape[0]
    alpha = (2 * depth) ** 0.25
    mod = _modulation(c, w_mod, b_mod)
    cos_t, sin_t = _rope_tables(positions)
    w_in_bf = w_in.astype(BF16)
    w_out_bf = w_out.astype(BF16)
    for l in range(depth):
        aq, ak, avt, cq, ck, cv, zg, zb, zd = _in_projection(x, mod[l], w_in_bf, l, cos_t, sin_t,
                                                               gla_wr[l], gla_br[l])
        ya = _moba(aq, ak, avt)
        w_gates = jnp.concatenate([_block_diag(lru_wa[l]), _block_diag(lru_wx[l])], axis=1).astype(BF16)
        b_gates = jnp.concatenate([lru_ba[l], lru_bx[l]])
        yb, yd = _recurrent_mixers(zb, conv_w[l], conv_b[l], w_gates, b_gates, lru_lam[l], zd, gla_gn[l])
        yc = _dilated(cq, ck, cv)
        x = _out_projection(ya, yb, yc, yd, zg, x, mod[l], w_out_bf, l, ln_g[l], ln_b[l], alpha)
    return x
```

```python
import functools

import jax
import jax.numpy as jnp
from jax import lax
from jax.experimental import pallas as pl
from jax.experimental.pallas import tpu as pltpu

F32 = jnp.float32
BF16 = jnp.bfloat16

HEAD_DIM = 64
MOBA_HEADS = 4
MOBA_BLOCK = 256
MOBA_TOPK = 3
LRU_WIDTH = 256
LRU_BLOCKS = 4
CONV_WIDTH = 4
LRU_C = 8.0
DIL_HEADS = 4
DIL_CONFIGS = ((128, 1), (512, 4), (2048, 16))
DIL_QBLOCK = 128
GLA_HEADS = 4
GLA_DK = 32
GLA_DV = 64
GLA_LOWRANK = 16
GLA_TAU = 16.0
GLA_CHUNK = 32
ROPE_THETA = 10000.0
LN_EPS = 1e-5
NEG = -1e30
LOG2E = 1.4426950408889634
MOBA_VROWS = HEAD_DIM + 16
GLA_SAFE_DECAY = 60.0
GLA_FAST_CHUNK = 128

A_W = MOBA_HEADS * HEAD_DIM
B_W = LRU_WIDTH
C_W = DIL_HEADS * HEAD_DIM
D_KW = GLA_HEADS * GLA_DK
D_W = GLA_HEADS * GLA_DV
OFF_AQ, OFF_AK, OFF_AV, OFF_AG = 0, A_W, 2 * A_W, 3 * A_W
OFF_BX = 4 * A_W
OFF_BG = OFF_BX + B_W
OFF_CQ = OFF_BG + B_W
OFF_CK, OFF_CV, OFF_CG = OFF_CQ + C_W, OFF_CQ + 2 * C_W, OFF_CQ + 3 * C_W
OFF_DQ = OFF_CQ + 4 * C_W
OFF_DK = OFF_DQ + D_KW
OFF_DV = OFF_DK + D_KW
OFF_DG = OFF_DV + D_W
OFF_DR = OFF_DG + D_W
D_IN = OFF_DR + GLA_LOWRANK

V7X_LANES = 128
V7X_SUBLANES = 8
V7X_VMEM_LIMIT_BYTES = 48 * 1024 * 1024

IN_TM = 512
IN_SPLIT = 2
OUT_TM = 1024
GLA_TG = 256
ROPE_TR = 512
ROPE_PACK = V7X_LANES // (HEAD_DIM // 2)
DIL_UNROLL = 2
DIL_TILE = DIL_QBLOCK * max(d for _, d in DIL_CONFIGS)


def _params(*sem):
    return pltpu.CompilerParams(dimension_semantics=sem, vmem_limit_bytes=V7X_VMEM_LIMIT_BYTES)


def _sigmoid(x):
    return 1.0 / (1.0 + jnp.exp(-x))


def _log_sigmoid(x):
    return jnp.minimum(x, 0.0) - jnp.log1p(jnp.exp(-jnp.abs(x)))


def _dot_nt(a, b, **kw):
    return lax.dot_general(a, b, (((1,), (1,)), ((), ())), preferred_element_type=F32, **kw)


def _mod_kernel(c_ref, w_ref, b_ref, o_ref):
    o_ref[0] = jnp.dot(c_ref[...].astype(BF16), w_ref[0].astype(BF16), preferred_element_type=F32) + b_ref[0]


def _modulation(c, w_mod, b_mod):
    depth, d, d3 = w_mod.shape
    b = c.shape[0]
    nt = d3 // d
    return pl.pallas_call(
        _mod_kernel,
        out_shape=jax.ShapeDtypeStruct((depth, b, d3), F32),
        grid=(depth, nt),
        in_specs=[pl.BlockSpec((b, d), lambda l, j: (0, 0)),
                  pl.BlockSpec((1, d, d), lambda l, j: (l, 0, j)),
                  pl.BlockSpec((1, 1, d), lambda l, j: (l, 0, j))],
        out_specs=pl.BlockSpec((1, b, d), lambda l, j: (l, 0, j)),
        compiler_params=_params("parallel", "parallel"),
        name="adaln_modulation",
    )(c, w_mod, b_mod.reshape(depth, 1, d3))


def _split3(x):
    hi = x.astype(BF16)
    mid = (x - hi.astype(F32)).astype(BF16)
    lo = (x - hi.astype(F32) - mid.astype(F32)).astype(BF16)
    return hi, mid, lo


def _rope_table_kernel(pos_ref, inv_ref, cos_ref, sin_ref):
    half = HEAD_DIM // 2
    rows = pos_ref.shape[1]
    ang = pos_ref[0].astype(F32) * inv_ref[...]
    src = lax.broadcasted_iota(jnp.int32, (V7X_LANES, V7X_LANES), 0)
    dst = lax.broadcasted_iota(jnp.int32, (V7X_LANES, V7X_LANES), 1)
    sign = jnp.where((dst % HEAD_DIM) < half, -1.0, 1.0)
    for table_ref, vals, signed in ((cos_ref, jnp.cos(ang), False), (sin_ref, jnp.sin(ang), True)):
        parts = _split3(vals)
        for r in range(ROPE_PACK):
            pick = src == r * half + dst % half
            spread = jnp.where(pick, sign if signed else 1.0, 0.0).astype(BF16)
            full = sum(jnp.dot(p, spread, preferred_element_type=F32) for p in parts)
            table_ref.at[0][pl.ds(r, rows, stride=ROPE_PACK), :] = full


def _rope_tables(positions):
    b, s = positions.shape
    half = HEAD_DIM // 2
    inv = ROPE_THETA ** (-jnp.arange(half, dtype=F32) / half)
    inv_row = jnp.tile(inv, ROPE_PACK).reshape(1, V7X_LANES)
    packed = jnp.repeat(positions.reshape(b, s // ROPE_PACK, ROPE_PACK), half, axis=-1)
    tab = jax.ShapeDtypeStruct((b, s, V7X_LANES), F32)
    out = pl.BlockSpec((1, ROPE_TR, V7X_LANES), lambda i, j: (i, j, 0))
    return pl.pallas_call(
        _rope_table_kernel,
        out_shape=(tab, tab),
        grid=(b, s // ROPE_TR),
        in_specs=[pl.BlockSpec((1, ROPE_TR // ROPE_PACK, V7X_LANES), lambda i, j: (i, j, 0)),
                  pl.BlockSpec((1, V7X_LANES), lambda i, j: (0, 0))],
        out_specs=(out, out),
        compiler_params=_params("parallel", "parallel"),
        name="rope_tables",
    )(packed, inv_row)


def _rope(z, cos, sin):
    lane = lax.broadcasted_iota(jnp.int32, (1, V7X_LANES), 1)
    first_half = (lane % HEAD_DIM) < (HEAD_DIM // 2)
    outs = []
    for s in range(z.shape[1] // V7X_LANES):
        zs = z[:, s * V7X_LANES:(s + 1) * V7X_LANES]
        swapped = jnp.where(first_half, pltpu.roll(zs, V7X_LANES - HEAD_DIM // 2, 1), pltpu.roll(zs, HEAD_DIM // 2, 1))
        outs.append(zs * cos + swapped * sin)
    return jnp.concatenate(outs, axis=-1)


def _inproj_kernel(x_ref, mod_ref, w_ref, cos_ref, sin_ref, wr_ref, br_ref,
                   aq_ref, ak_ref, avt_ref, cq_ref, ck_ref, cv_ref, zg_ref, zb_ref, zd_ref):
    d = x_ref.shape[2]
    shift = mod_ref[0, :, 0:d]
    scale = mod_ref[0, :, d:2 * d]
    q_scale = HEAD_DIM ** -0.5
    part_rows = x_ref.shape[1] // IN_SPLIT
    for part in range(IN_SPLIT):
        rows = slice(part * part_rows, (part + 1) * part_rows)
        x = x_ref[0, rows, :]
        mu = jnp.mean(x, axis=-1, keepdims=True)
        xc = x - mu
        var = jnp.mean(xc * xc, axis=-1, keepdims=True)
        xn = xc * lax.rsqrt(var + LN_EPS)
        u = (xn * (1.0 + scale) + shift).astype(BF16)

        def proj(off, width, u=u):
            return jnp.dot(u, w_ref[:, off:off + width], preferred_element_type=F32)

        cos = cos_ref[0, rows, :]
        sin = sin_ref[0, rows, :]
        aq_ref[0, rows, :] = (_rope(proj(OFF_AQ, A_W), cos, sin) * (q_scale * LOG2E)).astype(BF16)
        ak_ref[0, rows, :] = _rope(proj(OFF_AK, A_W), cos, sin).astype(BF16)
        vt = proj(OFF_AV, A_W).T.astype(BF16)
        ones = jnp.ones((MOBA_VROWS - HEAD_DIM, part_rows), BF16)
        for h in range(MOBA_HEADS):
            avt_ref[0, h * MOBA_VROWS:h * MOBA_VROWS + HEAD_DIM, rows] = vt[h * HEAD_DIM:(h + 1) * HEAD_DIM, :]
            avt_ref[0, h * MOBA_VROWS + HEAD_DIM:(h + 1) * MOBA_VROWS, rows] = ones
        for ref, z in ((cq_ref, _rope(proj(OFF_CQ, C_W), cos, sin) * (q_scale * LOG2E)),
                       (ck_ref, _rope(proj(OFF_CK, C_W), cos, sin)),
                       (cv_ref, proj(OFF_CV, C_W))):
            for half in range(C_W // V7X_LANES):
                ref[0, half, rows, :] = z[:, half * V7X_LANES:(half + 1) * V7X_LANES]
        for j, off in enumerate((OFF_AG, OFF_BG, OFF_CG, OFF_DG)):
            g = proj(off, A_W)
            zg_ref[0, rows, j * A_W:(j + 1) * A_W] = (g * _sigmoid(g)).astype(BF16)
        zb_ref[0, rows, :] = proj(OFF_BX, B_W)
        dqk = proj(OFF_DQ, 2 * D_KW)
        zd_ref[0, rows, 0:D_KW] = dqk[:, 0:D_KW] * (GLA_DK ** -0.5)
        zd_ref[0, rows, D_KW:2 * D_KW] = dqk[:, D_KW:]
        zd_ref[0, rows, 2 * D_KW:2 * D_KW + D_W] = proj(OFF_DV, D_W)
        dr = proj(OFF_DR, GLA_LOWRANK)
        la = jnp.dot(dr.astype(BF16), wr_ref[...].astype(BF16), preferred_element_type=F32) + br_ref[...]
        zd_ref[0, rows, 2 * D_KW + D_W:] = _log_sigmoid(la) * (1.0 / GLA_TAU)


def _in_projection(x, mod_l, w_in_bf, layer, cos_t, sin_t, gla_wr, gla_br):
    b, s, d = x.shape
    tm = IN_TM
    row3 = lambda w: pl.BlockSpec((1, tm, w), lambda i, j: (i, j, 0))
    const2 = lambda shp: pl.BlockSpec(shp, lambda i, j: (0, 0))
    per_layer = lambda shp: pl.BlockSpec((None,) + shp[1:], lambda i, j: (layer, 0, 0))
    halves = pl.BlockSpec((1, C_W // V7X_LANES, tm, V7X_LANES), lambda i, j: (i, 0, j, 0))
    out_shape = (
        jax.ShapeDtypeStruct((b, s, A_W), BF16),
        jax.ShapeDtypeStruct((b, s, A_W), BF16),
        jax.ShapeDtypeStruct((b, MOBA_HEADS * MOBA_VROWS, s), BF16),
        jax.ShapeDtypeStruct((b, C_W // V7X_LANES, s, V7X_LANES), F32),
        jax.ShapeDtypeStruct((b, C_W // V7X_LANES, s, V7X_LANES), F32),
        jax.ShapeDtypeStruct((b, C_W // V7X_LANES, s, V7X_LANES), F32),
        jax.ShapeDtypeStruct((b, s, 4 * A_W), BF16),
        jax.ShapeDtypeStruct((b, s, B_W), F32),
        jax.ShapeDtypeStruct((b, s, 2 * D_KW + D_W + D_KW), F32),
    )
    out_specs = (row3(A_W), row3(A_W), pl.BlockSpec((1, MOBA_HEADS * MOBA_VROWS, tm), lambda i, j: (i, 0, j)),
                 halves, halves, halves, row3(4 * A_W), row3(B_W), row3(2 * D_KW + D_W + D_KW))
    return pl.pallas_call(
        _inproj_kernel,
        out_shape=out_shape,
        grid=(b, s // tm),
        in_specs=[row3(d),
                  pl.BlockSpec((1, 1, mod_l.shape[-1]), lambda i, j: (i, 0, 0)),
                  per_layer(w_in_bf.shape),
                  row3(V7X_LANES), row3(V7X_LANES),
                  const2(gla_wr.shape), const2((1, D_KW))],
        out_specs=out_specs,
        compiler_params=_params("parallel", "parallel"),
        name="in_projection",
    )(x, mod_l.reshape(b, 1, -1), w_in_bf, cos_t, sin_t, gla_wr, gla_br.reshape(1, D_KW))


def _moba_kernel(q_ref, k_ref, vt_ref, o_ref, kmean_ref, qm_ref, unsel_ref, s_ref, p_ref, alpha_ref, acc_ref):
    blk = MOBA_BLOCK
    nb = k_ref.shape[1] // blk
    n = pl.program_id(1)

    @pl.when(n == 0)
    def _():
        for j in range(nb):
            kj = k_ref[0, j * blk:(j + 1) * blk, :].astype(F32)
            kmean_ref[j:j + 1, :] = jnp.mean(kj, axis=0, keepdims=True)

    q = q_ref[0]
    hq = MOBA_HEADS * blk
    lane_head = lax.broadcasted_iota(jnp.int32, (1, A_W), 1) // HEAD_DIM
    km = kmean_ref[...]
    km_hi = km.astype(BF16)
    km_lo = (km - km_hi.astype(F32)).astype(BF16)
    key_idx = lax.broadcasted_iota(jnp.int32, (blk, blk), 0)
    qry_idx = lax.broadcasted_iota(jnp.int32, (blk, blk), 1)
    causal = key_idx <= qry_idx

    vr = MOBA_VROWS
    vth = lambda h, j: vt_ref[0, h * vr:(h + 1) * vr, pl.ds(pl.multiple_of(j * blk, blk), blk)]
    kblk = lambda j: k_ref[0, pl.ds(pl.multiple_of(j * blk, blk), blk), :]

    for h in range(MOBA_HEADS):
        qm_ref[h * blk:(h + 1) * blk, :] = jnp.where(lane_head == h, q, jnp.zeros_like(q))
    scores = lambda j: _dot_nt(kblk(j), qm_ref[...])

    blk_id = lax.broadcasted_iota(jnp.int32, (nb, MOBA_HEADS * blk), 0)
    own = _dot_nt(jnp.concatenate([kblk(n), km_hi, km_lo], axis=0), qm_ref[...])
    s_own = own[0:blk, :]
    gate = jnp.where(blk_id < n, own[blk:blk + nb, :] + own[blk + nb:blk + 2 * nb, :], NEG)
    s_ref[0, :, 0:hq] = scores(0)
    rank = jnp.zeros(gate.shape, jnp.int32)
    for i in range(nb):
        gi = gate[i:i + 1, :]
        tie = (blk_id > i).astype(jnp.int32)
        rank = rank + jnp.where(gi > gate, 1, jnp.where(gi == gate, tie, 0))
    unsel_ref[...] = jnp.where((rank < MOBA_TOPK) & (blk_id < n), 0.0, 1.0)

    def accumulate(slot, j_blk):
        for h in range(MOBA_HEADS):
            cols = slice(h * blk, (h + 1) * blk)
            acc_ref[h * vr:(h + 1) * vr, :] = (alpha_ref[slot, :, cols] * acc_ref[h * vr:(h + 1) * vr, :]
                                              + jnp.dot(vth(h, j_blk), p_ref[slot, :, cols],
                                                        preferred_element_type=F32))

    acc_ref[...] = jnp.zeros_like(acc_ref)
    alpha_ref[1] = jnp.ones(alpha_ref.shape[1:], F32)
    s = jnp.where(jnp.concatenate([causal] * MOBA_HEADS, axis=1), s_own, NEG)
    m_init = jnp.max(s, axis=0, keepdims=True)
    p_ref[1, :, 0:hq] = jnp.exp2(s - m_init).astype(BF16)

    def step(j, j_prev, slot, m, prefetch=True):
        s = s_ref[slot, :, 0:hq]
        accumulate(1 - slot, j_prev)
        if prefetch:
            s_ref[1 - slot, :, 0:hq] = scores(jnp.minimum(j + 1, nb - 1))
        unsel = unsel_ref[pl.ds(j, 1), :] > 0.5
        m_new = jnp.maximum(m, jnp.where(unsel, NEG, jnp.max(s, axis=0, keepdims=True)))
        alpha_ref[slot] = jnp.exp2(m - m_new)
        p_ref[slot, :, 0:hq] = jnp.exp2(s - jnp.where(unsel, -NEG, m_new)).astype(BF16)
        return m_new

    def body(i, m):
        m = step(2 * i, jnp.where(i == 0, n, 2 * i - 1), 0, m)
        return step(2 * i + 1, 2 * i, 1, m)

    m_pairs = lax.fori_loop(0, n // 2, body, m_init)

    @pl.when(n % 2 == 1)
    def _():
        step(n - 1, jnp.where(n == 1, n, n - 2), 0, m_pairs, prefetch=False)
        accumulate(0, n - 1)

    @pl.when(n % 2 == 0)
    def _():
        accumulate(1, jnp.where(n == 0, n, n - 1))
    outs = []
    for h in range(MOBA_HEADS):
        a = acc_ref[h * vr:(h + 1) * vr, :]
        outs.append(a[0:HEAD_DIM, :] / a[HEAD_DIM:HEAD_DIM + 1, :])
    o_ref[0] = jnp.concatenate(outs, axis=0).T.astype(o_ref.dtype)


def _moba(aq, ak, avt):
    b, s, _ = aq.shape
    blk = MOBA_BLOCK
    return pl.pallas_call(
        _moba_kernel,
        out_shape=jax.ShapeDtypeStruct((b, s, A_W), BF16),
        grid=(b, s // blk),
        in_specs=[pl.BlockSpec((1, blk, A_W), lambda i, n: (i, n, 0)),
                  pl.BlockSpec((1, s, A_W), lambda i, n: (i, 0, 0)),
                  pl.BlockSpec((1, MOBA_HEADS * MOBA_VROWS, s), lambda i, n: (i, 0, 0))],
        out_specs=pl.BlockSpec((1, blk, A_W), lambda i, n: (i, n, 0)),
        scratch_shapes=[pltpu.VMEM((s // blk, A_W), F32),
                        pltpu.VMEM((MOBA_HEADS * blk, A_W), BF16),
                        pltpu.VMEM((s // blk, MOBA_HEADS * blk), F32),
                        pltpu.VMEM((2, blk, MOBA_HEADS * blk + V7X_LANES), F32),
                        pltpu.VMEM((2, blk, MOBA_HEADS * blk + V7X_LANES), BF16),
                        pltpu.VMEM((2, 1, MOBA_HEADS * blk), F32),
                        pltpu.VMEM((MOBA_HEADS * MOBA_VROWS, blk), F32)],
        compiler_params=_params("arbitrary", "arbitrary"),
        name="moba_attention",
    )(aq, ak, avt)


def _lru_reset(xs_ref, h_ref):
    xs_ref[0:V7X_SUBLANES, :] = jnp.zeros((V7X_SUBLANES, B_W), F32)
    h_ref[...] = jnp.zeros_like(h_ref)


def _lru_tile(x_ref, cw_ref, cb_ref, wg_ref, bg_ref, lam_ref, o_ref, xs_ref, h_ref):
    tl = x_ref.shape[1]
    halo = V7X_SUBLANES
    x = x_ref[0]
    xs_ref[halo:, :] = x
    y = cb_ref[...] + cw_ref[CONV_WIDTH - 1:CONV_WIDTH, :] * x
    for k in range(CONV_WIDTH - 1):
        y = y + cw_ref[k:k + 1, :] * xs_ref[pl.ds(halo - (CONV_WIDTH - 1) + k, tl), :]
    xs_ref[0:halo, :] = x[tl - halo:, :]

    gates = jnp.dot(y.astype(BF16), wg_ref[...], preferred_element_type=F32) + bg_ref[...]
    r = _sigmoid(gates[:, 0:B_W])
    ig = _sigmoid(gates[:, B_W:])
    lam = lam_ref[...]
    softplus_neg_lam = jnp.maximum(-lam, 0.0) + jnp.log1p(jnp.exp(-jnp.abs(lam)))
    log_a = (-LRU_C) * r * softplus_neg_lam
    a = jnp.exp(log_a)
    u = jnp.sqrt(-jnp.tanh(log_a) * (a * a + 1.0)) * (ig * y)

    row = lax.broadcasted_iota(jnp.int32, (tl, 1), 0)
    step = 1
    while step < tl:
        if step % V7X_SUBLANES:
            keep = row >= step
            a_prev = jnp.where(keep, pltpu.roll(a, step, 0), 1.0)
            u_prev = jnp.where(keep, pltpu.roll(u, step, 0), 0.0)
            u = u + a * u_prev
            a = a * a_prev
        else:
            u = jnp.concatenate([u[:step], u[step:] + a[step:] * u[:tl - step]], axis=0)
            a = jnp.concatenate([a[:step], a[step:] * a[:tl - step]], axis=0)
        step *= 2
    h = u + a * h_ref[...]
    h_ref[...] = h[tl - 1:tl, :]
    o_ref[0] = h.astype(o_ref.dtype)


def _dilated_kernel(q_ref, k_ref, v_ref, o_ref, oacc_ref, lacc_ref):
    qb = DIL_QBLOCK
    tile = o_ref.shape[1]
    t0 = pl.program_id(1) * tile
    halves = range(C_W // V7X_LANES)

    def load(ref, rows):
        return jnp.concatenate([ref.at[half][rows, :] for half in halves], axis=-1)

    def store(ref, rows, value):
        for half in halves:
            ref.at[half][rows, :] = value[:, half * V7X_LANES:(half + 1) * V7X_LANES]

    lane_head = lax.broadcasted_iota(jnp.int32, (1, C_W), 1) // HEAD_DIM
    qi = lax.broadcasted_iota(jnp.int32, (qb, 2 * qb), 0)
    kj = lax.broadcasted_iota(jnp.int32, (qb, 2 * qb), 1)

    for ci, (window, d) in enumerate(DIL_CONFIGS):
        n_steps = window // d
        per_class = tile // (qb * d)
        first, last = ci == 0, ci == len(DIL_CONFIGS) - 1

        def units(i, carry, d=d, n_steps=n_steps, per_class=per_class, first=first, last=last):
            group = []
            for k in range(DIL_UNROLL):
                u = i * DIL_UNROLL + k
                r = u // per_class
                base = t0 // d + (u % per_class) * qb
                start = jnp.maximum(base - qb, 0)
                if d == 1:
                    tokens = lambda m0, count, r=r: pl.ds(pl.multiple_of(m0, qb), count)
                else:
                    tokens = lambda m0, count, r=r: pl.ds(m0 * d + r, count, stride=d)
                q = load(q_ref, tokens(base, qb))
                kk = load(k_ref, tokens(start, 2 * qb)).astype(BF16)
                vv = load(v_ref, tokens(start, 2 * qb)).astype(BF16)
                dist = (base - start) + qi - kj
                valid = (dist >= 0) & (dist <= n_steps)
                q_bf = q.astype(BF16)
                q_stack = jnp.concatenate([jnp.where(lane_head == h, q_bf, jnp.zeros_like(q_bf))
                                           for h in range(DIL_HEADS)], axis=0)
                s = _dot_nt(q_stack, kk)
                group.append((s, valid, vv, tokens(base - t0 // d, qb)))
            soft = []
            for s, valid, vv, local in group:
                s = jnp.where(jnp.concatenate([valid] * DIL_HEADS, axis=0), s, NEG)
                m = jnp.max(s, axis=-1, keepdims=True)
                e = jnp.exp2(s - m)
                soft.append((e.astype(BF16), m, jnp.sum(e, axis=-1, keepdims=True)))
            pvs = [jnp.dot(e, vv, preferred_element_type=F32)
                   for (e, _, _), (_, _, vv, _) in zip(soft, group)]
            for pv, (_, m, l), (_, _, _, local) in zip(pvs, soft, group):
                o_all = pv / l
                lse_all = m + jnp.log2(l)
                out = o_all[0:qb, :]
                lse = jnp.broadcast_to(lse_all[0:qb, :], (qb, C_W))
                for h in range(1, DIL_HEADS):
                    out = jnp.where(lane_head == h, o_all[h * qb:(h + 1) * qb, :], out)
                    lse = jnp.where(lane_head == h, lse_all[h * qb:(h + 1) * qb, :], lse)
                if not first:
                    o_prev = load(oacc_ref, local)
                    l_prev = load(lacc_ref, local)
                    top = jnp.maximum(l_prev, lse)
                    w_prev = jnp.exp2(l_prev - top)
                    w_cur = jnp.exp2(lse - top)
                    out = (w_prev * o_prev + w_cur * out) / (w_prev + w_cur)
                    lse = top + jnp.log2(w_prev + w_cur)
                if last:
                    store(o_ref, local, out)
                else:
                    store(oacc_ref, local, out)
                    store(lacc_ref, local, lse)
            return carry

        lax.fori_loop(0, tile // (qb * DIL_UNROLL), units, 0)


def _dilated(cq, ck, cv):
    b, nh, s, w = cq.shape
    tile = DIL_TILE
    whole = pl.BlockSpec((None, nh, s, w), lambda i, j: (i, 0, 0, 0))
    return pl.pallas_call(
        _dilated_kernel,
        out_shape=jax.ShapeDtypeStruct((b, nh, s, w), F32),
        grid=(b, s // tile),
        in_specs=[whole, whole, whole],
        out_specs=pl.BlockSpec((None, nh, tile, w), lambda i, j: (i, 0, j, 0)),
        scratch_shapes=[pltpu.VMEM((nh, tile, w), F32), pltpu.VMEM((nh, tile, w), F32)],
        compiler_params=_params("parallel", "arbitrary"),
        name="dilated_attention",
    )(cq, ck, cv)


def _gla_tile(z_ref, gn_ref, o_ref, state_ref, b_ref, q_ref, k_ref, v_ref, t_ref, y_ref, side_work):
    c = GLA_CHUNK
    tg = z_ref.shape[1]
    nch = tg // c
    i = pl.program_id(1)

    @pl.when(i == 0)
    def _():
        state_ref[...] = jnp.zeros_like(state_ref)

    qs = z_ref[0, :, 0:D_KW]
    ks = z_ref[0, :, D_KW:2 * D_KW]
    vs = z_ref[0, :, 2 * D_KW:2 * D_KW + D_W]
    la = z_ref[0, :, 2 * D_KW + D_W:]

    v_bf = vs.astype(BF16)
    ri = lax.broadcasted_iota(jnp.int32, (tg, tg), 0)
    ci = lax.broadcasted_iota(jnp.int32, (tg, tg), 1)

    def decays(c):
        chunk_causal = (ri // c == ci // c) & (ci <= ri)
        tri = jnp.where(chunk_causal, 1.0, 0.0).astype(BF16)
        b = sum(jnp.dot(tri, part, preferred_element_type=F32) for part in _split3(la))
        b_last = jnp.broadcast_to(b.reshape(tg // c, c, D_KW)[:, c - 1:c, :], (tg // c, c, D_KW)).reshape(tg, D_KW)
        return chunk_causal, b, b_last

    def through_state(b, b_last, c):
        same_head_t = (lax.broadcasted_iota(jnp.int32, (D_W, D_KW), 0) // GLA_DV
                       == lax.broadcasted_iota(jnp.int32, (D_W, D_KW), 1) // GLA_DK)
        q_in_bf = (qs * jnp.exp(b)).astype(BF16)
        k_dec_bf = (ks * jnp.exp(b_last - b)).astype(BF16)
        st = state_ref[...]
        inter = []
        for ch in range(tg // c):
            r0 = ch * c
            inter.append(_dot_nt(q_in_bf[r0:r0 + c, :], st.astype(BF16)))
            kvt = lax.dot_general(v_bf[r0:r0 + c, :], k_dec_bf[r0:r0 + c, :], (((0,), (0,)), ((), ())),
                                  preferred_element_type=F32)
            st = st * jnp.exp(b_last[r0:r0 + 1, :]) + jnp.where(same_head_t, kvt, 0.0)
        state_ref[...] = st
        y_ref[...] = y_ref[...] + jnp.concatenate(inter, axis=0)

    safe = jnp.min(la) * GLA_FAST_CHUNK > -GLA_SAFE_DECAY

    @pl.when(safe)
    def _():
        side_work()
        chunk_causal, b, b_last = decays(GLA_FAST_CHUNK)
        q_in = qs * jnp.exp(b)
        k_grow = (ks * jnp.exp(-b)).astype(BF16)
        dk_head = lax.broadcasted_iota(jnp.int32, (1, D_KW), 1) // GLA_DK
        q_stack = jnp.concatenate([jnp.where(dk_head == h, q_in, 0.0) for h in range(GLA_HEADS)], axis=0)
        a = _dot_nt(q_stack.astype(BF16), k_grow)
        a = jnp.where(jnp.concatenate([chunk_causal] * GLA_HEADS, axis=0), a, 0.0)
        av = jnp.dot(a.astype(BF16), v_bf, preferred_element_type=F32)
        dv_head = lax.broadcasted_iota(jnp.int32, (1, D_W), 1) // GLA_DV
        y = av[0:tg, :]
        for h in range(1, GLA_HEADS):
            y = jnp.where(dv_head == h, av[h * tg:(h + 1) * tg, :], y)
        y_ref[...] = y
        through_state(b, b_last, GLA_FAST_CHUNK)

    @pl.when(jnp.logical_not(safe))
    def _():
        side_work()
        _, b, b_last = decays(c)
        b_ref[...] = b
        q_ref[...] = qs
        k_ref[...] = ks
        v_ref[...] = vs
        head_sum = (lax.broadcasted_iota(jnp.int32, (D_KW, D_W), 0) // GLA_DK
                    == lax.broadcasted_iota(jnp.int32, (D_KW, D_W), 1) // GLA_DV).astype(BF16)
        jrow = lax.broadcasted_iota(jnp.int32, (c, D_KW), 0)

        def chunk(ch, carry):
            r0 = pl.multiple_of(ch * c, c)
            bch = b_ref[pl.ds(r0, c), :]
            kch = k_ref[pl.ds(r0, c), :]

            def row_terms(ii, carry2):
                bi = b_ref[pl.ds(r0 + ii, 1), :]
                qi = q_ref[pl.ds(r0 + ii, 1), :]
                t = jnp.exp(jnp.minimum(bi - bch, 0.0)) * kch * qi
                t_ref[pl.ds(pl.multiple_of(ii * c, c), c), :] = jnp.where(jrow <= ii, t, 0.0)
                return carry2

            lax.fori_loop(0, c, row_terms, 0)
            a_b = jnp.dot(t_ref[...].astype(BF16), head_sum, preferred_element_type=F32)
            y_ref[pl.ds(r0, c), :] = jnp.sum(a_b.reshape(c, c, D_W) * v_ref[pl.ds(r0, c), :][None, :, :], axis=1)
            return carry

        lax.fori_loop(0, nch, chunk, 0)
        through_state(b, b_last, c)

    y = y_ref[...]

    e_head = jnp.where(lax.broadcasted_iota(jnp.int32, (D_W, D_W), 0) // GLA_DV
                       == lax.broadcasted_iota(jnp.int32, (D_W, D_W), 1) // GLA_DV, 1.0 / GLA_DV, 0.0).astype(BF16)
    yy = y * y
    yy_hi = yy.astype(BF16)
    yy_lo = (yy - yy_hi.astype(F32)).astype(BF16)
    ms = (jnp.dot(yy_hi, e_head, preferred_element_type=F32) + jnp.dot(yy_lo, e_head, preferred_element_type=F32))
    o_ref[0] = (y * lax.rsqrt(ms + LN_EPS) * gn_ref[...]).astype(o_ref.dtype)


def _recurrent_kernel(zb_ref, cw_ref, cb_ref, wg_ref, bg_ref, lam_ref, zd_ref, gn_ref, yb_ref, yd_ref,
                      xs_ref, h_ref, state_ref, b_ref, q_ref, k_ref, v_ref, t_ref, y_ref):
    @pl.when(pl.program_id(1) == 0)
    def _():
        _lru_reset(xs_ref, h_ref)

    lru = functools.partial(_lru_tile, zb_ref, cw_ref, cb_ref, wg_ref, bg_ref, lam_ref, yb_ref, xs_ref, h_ref)
    _gla_tile(zd_ref, gn_ref, yd_ref, state_ref, b_ref, q_ref, k_ref, v_ref, t_ref, y_ref, lru)


def _recurrent_mixers(zb, conv_w, conv_b, w_gates, b_gates, lam, zd, gla_gn):
    b, s, w = zb.shape
    wd = zd.shape[2]
    tg = GLA_TG
    gn_row = jnp.tile(gla_gn.astype(F32), GLA_HEADS).reshape(1, D_W)
    const = lambda shp: pl.BlockSpec(shp, lambda i, j: (0, 0))
    rows = lambda width: pl.BlockSpec((1, tg, width), lambda i, j: (i, j, 0))
    return pl.pallas_call(
        _recurrent_kernel,
        out_shape=(jax.ShapeDtypeStruct((b, s, w), BF16), jax.ShapeDtypeStruct((b, s, D_W), BF16)),
        grid=(b, s // tg),
        in_specs=[rows(w), const(conv_w.shape), const((1, w)), const(w_gates.shape), const((1, 2 * w)),
                  const((1, w)), rows(wd), const((1, D_W))],
        out_specs=(rows(w), rows(D_W)),
        scratch_shapes=[pltpu.VMEM((tg + V7X_SUBLANES, w), F32),
                        pltpu.VMEM((1, w), F32),
                        pltpu.VMEM((D_W, D_KW), F32),
                        pltpu.VMEM((tg, D_KW), F32),
                        pltpu.VMEM((tg, D_KW), F32),
                        pltpu.VMEM((tg, D_KW), F32),
                        pltpu.VMEM((tg, D_W), F32),
                        pltpu.VMEM((GLA_CHUNK * GLA_CHUNK, D_KW), F32),
                        pltpu.VMEM((tg, D_W), F32)],
        compiler_params=_params("arbitrary", "arbitrary"),
        name="rg_lru_and_gla",
    )(zb, conv_w, conv_b.reshape(1, w), w_gates, b_gates.reshape(1, 2 * w), lam.reshape(1, w), zd, gn_row)


def _outproj_kernel(ya_ref, yb_ref, yc_ref, yd_ref, zg_ref, x_ref, mod_ref, w_ref, g_ref, b_ref, o_ref, *, alpha):
    d = x_ref.shape[2]
    y = jnp.zeros((x_ref.shape[1], d), F32)
    yc = jnp.concatenate([yc_ref[0, half] for half in range(yc_ref.shape[1])], axis=-1)
    for j, yj in enumerate((ya_ref[0], yb_ref[0], yc, yd_ref[0])):
        mix = (yj.astype(F32) * zg_ref[0, :, j * A_W:(j + 1) * A_W].astype(F32)).astype(BF16)
        y = y + jnp.dot(mix, w_ref[j * A_W:(j + 1) * A_W, :], preferred_element_type=F32)
    gate = mod_ref[0, :, 2 * d:3 * d]
    r = alpha * x_ref[0] + (1.0 + gate) * y
    mu = jnp.mean(r, axis=-1, keepdims=True)
    rc = r - mu
    var = jnp.mean(rc * rc, axis=-1, keepdims=True)
    o_ref[0] = rc * lax.rsqrt(var + LN_EPS) * g_ref[...] + b_ref[...]


def _out_projection(ya, yb, yc, yd, zg, x, mod_l, w_out_bf, layer, ln_g, ln_b, alpha):
    b, s, d = x.shape
    tm = OUT_TM
    row3 = lambda w: pl.BlockSpec((1, tm, w), lambda i, j: (i, j, 0))
    const2 = lambda shp: pl.BlockSpec(shp, lambda i, j: (0, 0))
    return pl.pallas_call(
        functools.partial(_outproj_kernel, alpha=alpha),
        out_shape=jax.ShapeDtypeStruct((b, s, d), F32),
        grid=(b, s // tm),
        in_specs=[row3(A_W), row3(B_W),
                  pl.BlockSpec((1, yc.shape[1], tm, yc.shape[3]), lambda i, j: (i, 0, j, 0)),
                  row3(D_W), row3(4 * A_W), row3(d),
                  pl.BlockSpec((1, 1, mod_l.shape[-1]), lambda i, j: (i, 0, 0)),
                  pl.BlockSpec((None,) + w_out_bf.shape[1:], lambda i, j: (layer, 0, 0)),
                  const2((1, d)), const2((1, d))],
        out_specs=row3(d),
        compiler_params=_params("parallel", "parallel"),
        name="out_projection",
    )(ya, yb, yc, yd, zg, x, mod_l.reshape(b, 1, -1), w_out_bf, ln_g.reshape(1, d), ln_b.reshape(1, d))


def _block_diag(w):
    g, bd, _ = w.shape
    eye = jnp.eye(g, dtype=w.dtype)
    return (eye[:, None, :, None] * w[:, :, None, :]).reshape(g * bd, g * bd)


def kernel(x, c, positions, w_mod, b_mod, w_in, conv_w, conv_b, lru_wa, lru_ba, lru_wx, lru_bx, lru_lam,
           gla_wr, gla_br, gla_gn, w_out, ln_g, ln_b):
    depth = w_mod.shape[0]
    alpha = (2 * depth) ** 0.25
    mod = _modulation(c, w_mod, b_mod)
    cos_t, sin_t = _rope_tables(positions)
    w_in_bf = w_in.astype(BF16)
    w_out_bf = w_out.astype(BF16)
    for l in range(depth):
        aq, ak, avt, cq, ck, cv, zg, zb, zd = _in_projection(x, mod[l], w_in_bf, l, cos_t, sin_t,
                                                               gla_wr[l], gla_br[l])
        ya = _moba(aq, ak, avt)
        w_gates = jnp.concatenate([_block_diag(lru_wa[l]), _block_diag(lru_wx[l])], axis=1).astype(BF16)
        b_gates = jnp.concatenate([lru_ba[l], lru_bx[l]])
        yb, yd = _recurrent_mixers(zb, conv_w[l], conv_b[l], w_gates, b_gates, lru_lam[l], zd, gla_gn[l])
        yc = _dilated(cq, ck, cv)
        x = _out_projection(ya, yb, yc, yd, zg, x, mod[l], w_out_bf, l, ln_g[l], ln_b[l], alpha)
    return x
```

```python
import functools

import jax
import jax.numpy as jnp
from jax import lax
from jax.experimental import pallas as pl
from jax.experimental.pallas import tpu as pltpu

F32 = jnp.float32
BF16 = jnp.bfloat16

HEAD_DIM = 64
MOBA_HEADS = 4
MOBA_BLOCK = 256
MOBA_TOPK = 3
LRU_WIDTH = 256
LRU_BLOCKS = 4
CONV_WIDTH = 4
LRU_C = 8.0
DIL_HEADS = 4
DIL_CONFIGS = ((128, 1), (512, 4), (2048, 16))
DIL_QBLOCK = 128
GLA_HEADS = 4
GLA_DK = 32
GLA_DV = 64
GLA_LOWRANK = 16
GLA_TAU = 16.0
GLA_CHUNK = 32
ROPE_THETA = 10000.0
LN_EPS = 1e-5
NEG = -1e30
LOG2E = 1.4426950408889634
MOBA_VROWS = HEAD_DIM + 16
GLA_SAFE_DECAY = 60.0
GLA_FAST_CHUNK = 128

A_W = MOBA_HEADS * HEAD_DIM
B_W = LRU_WIDTH
C_W = DIL_HEADS * HEAD_DIM
D_KW = GLA_HEADS * GLA_DK
D_W = GLA_HEADS * GLA_DV
OFF_AQ, OFF_AK, OFF_AV, OFF_AG = 0, A_W, 2 * A_W, 3 * A_W
OFF_BX = 4 * A_W
OFF_BG = OFF_BX + B_W
OFF_CQ = OFF_BG + B_W
OFF_CK, OFF_CV, OFF_CG = OFF_CQ + C_W, OFF_CQ + 2 * C_W, OFF_CQ + 3 * C_W
OFF_DQ = OFF_CQ + 4 * C_W
OFF_DK = OFF_DQ + D_KW
OFF_DV = OFF_DK + D_KW
OFF_DG = OFF_DV + D_W
OFF_DR = OFF_DG + D_W
D_IN = OFF_DR + GLA_LOWRANK

V7X_LANES = 128
V7X_SUBLANES = 8
V7X_VMEM_LIMIT_BYTES = 48 * 1024 * 1024

IN_TM = 512
IN_SPLIT = 2
OUT_TM = 1024
GLA_TG = 256
ROPE_TR = 512
ROPE_PACK = V7X_LANES // (HEAD_DIM // 2)
DIL_UNROLL = 2
DIL_TILE = DIL_QBLOCK * max(d for _, d in DIL_CONFIGS)


def _params(*sem):
    return pltpu.CompilerParams(dimension_semantics=sem, vmem_limit_bytes=V7X_VMEM_LIMIT_BYTES)


def _sigmoid(x):
    return 1.0 / (1.0 + jnp.exp(-x))


def _log_sigmoid(x):
    return jnp.minimum(x, 0.0) - jnp.log1p(jnp.exp(-jnp.abs(x)))


def _dot_nt(a, b, **kw):
    return lax.dot_general(a, b, (((1,), (1,)), ((), ())), preferred_element_type=F32, **kw)


def _mod_kernel(c_ref, w_ref, b_ref, o_ref):
    o_ref[0] = jnp.dot(c_ref[...].astype(BF16), w_ref[0].astype(BF16), preferred_element_type=F32) + b_ref[0]


def _modulation(c, w_mod, b_mod):
    depth, d, d3 = w_mod.shape
    b = c.shape[0]
    nt = d3 // d
    return pl.pallas_call(
        _mod_kernel,
        out_shape=jax.ShapeDtypeStruct((depth, b, d3), F32),
        grid=(depth, nt),
        in_specs=[pl.BlockSpec((b, d), lambda l, j: (0, 0)),
                  pl.BlockSpec((1, d, d), lambda l, j: (l, 0, j)),
                  pl.BlockSpec((1, 1, d), lambda l, j: (l, 0, j))],
        out_specs=pl.BlockSpec((1, b, d), lambda l, j: (l, 0, j)),
        compiler_params=_params("parallel", "parallel"),
        name="adaln_modulation",
    )(c, w_mod, b_mod.reshape(depth, 1, d3))


def _split3(x):
    hi = x.astype(BF16)
    mid = (x - hi.astype(F32)).astype(BF16)
    lo = (x - hi.astype(F32) - mid.astype(F32)).astype(BF16)
    return hi, mid, lo


def _rope_table_kernel(pos_ref, inv_ref, cos_ref, sin_ref):
    half = HEAD_DIM // 2
    rows = pos_ref.shape[1]
    ang = pos_ref[0].astype(F32) * inv_ref[...]
    src = lax.broadcasted_iota(jnp.int32, (V7X_LANES, V7X_LANES), 0)
    dst = lax.broadcasted_iota(jnp.int32, (V7X_LANES, V7X_LANES), 1)
    sign = jnp.where((dst % HEAD_DIM) < half, -1.0, 1.0)
    for table_ref, vals, signed in ((cos_ref, jnp.cos(ang), False), (sin_ref, jnp.sin(ang), True)):
        parts = _split3(vals)
        for r in range(ROPE_PACK):
            pick = src == r * half + dst % half
            spread = jnp.where(pick, sign if signed else 1.0, 0.0).astype(BF16)
            full = sum(jnp.dot(p, spread, preferred_element_type=F32) for p in parts)
            table_ref.at[0][pl.ds(r, rows, stride=ROPE_PACK), :] = full


def _rope_tables(positions):
    b, s = positions.shape
    half = HEAD_DIM // 2
    inv = ROPE_THETA ** (-jnp.arange(half, dtype=F32) / half)
    inv_row = jnp.tile(inv, ROPE_PACK).reshape(1, V7X_LANES)
    packed = jnp.repeat(positions.reshape(b, s // ROPE_PACK, ROPE_PACK), half, axis=-1)
    tab = jax.ShapeDtypeStruct((b, s, V7X_LANES), F32)
    out = pl.BlockSpec((1, ROPE_TR, V7X_LANES), lambda i, j: (i, j, 0))
    return pl.pallas_call(
        _rope_table_kernel,
        out_shape=(tab, tab),
        grid=(b, s // ROPE_TR),
        in_specs=[pl.BlockSpec((1, ROPE_TR // ROPE_PACK, V7X_LANES), lambda i, j: (i, j, 0)),
                  pl.BlockSpec((1, V7X_LANES), lambda i, j: (0, 0))],
        out_specs=(out, out),
        compiler_params=_params("parallel", "parallel"),
        name="rope_tables",
    )(packed, inv_row)


def _rope(z, cos, sin):
    lane = lax.broadcasted_iota(jnp.int32, (1, V7X_LANES), 1)
    first_half = (lane % HEAD_DIM) < (HEAD_DIM // 2)
    outs = []
    for s in range(z.shape[1] // V7X_LANES):
        zs = z[:, s * V7X_LANES:(s + 1) * V7X_LANES]
        swapped = jnp.where(first_half, pltpu.roll(zs, V7X_LANES - HEAD_DIM // 2, 1), pltpu.roll(zs, HEAD_DIM // 2, 1))
        outs.append(zs * cos + swapped * sin)
    return jnp.concatenate(outs, axis=-1)


def _inproj_kernel(x_ref, mod_ref, w_ref, cos_ref, sin_ref, wr_ref, br_ref,
                   aq_ref, ak_ref, avt_ref, cq_ref, ck_ref, cv_ref, zg_ref, zb_ref, zd_ref):
    d = x_ref.shape[2]
    shift = mod_ref[0, :, 0:d]
    scale = mod_ref[0, :, d:2 * d]
    q_scale = HEAD_DIM ** -0.5
    part_rows = x_ref.shape[1] // IN_SPLIT
    for part in range(IN_SPLIT):
        rows = slice(part * part_rows, (part + 1) * part_rows)
        x = x_ref[0, rows, :]
        mu = jnp.mean(x, axis=-1, keepdims=True)
        xc = x - mu
        var = jnp.mean(xc * xc, axis=-1, keepdims=True)
        xn = xc * lax.rsqrt(var + LN_EPS)
        u = (xn * (1.0 + scale) + shift).astype(BF16)

        def proj(off, width, u=u):
            return jnp.dot(u, w_ref[:, off:off + width], preferred_element_type=F32)

        cos = cos_ref[0, rows, :]
        sin = sin_ref[0, rows, :]
        aq_ref[0, rows, :] = (_rope(proj(OFF_AQ, A_W), cos, sin) * (q_scale * LOG2E)).astype(BF16)
        ak_ref[0, rows, :] = _rope(proj(OFF_AK, A_W), cos, sin).astype(BF16)
        vt = proj(OFF_AV, A_W).T.astype(BF16)
        ones = jnp.ones((MOBA_VROWS - HEAD_DIM, part_rows), BF16)
        for h in range(MOBA_HEADS):
            avt_ref[0, h * MOBA_VROWS:h * MOBA_VROWS + HEAD_DIM, rows] = vt[h * HEAD_DIM:(h + 1) * HEAD_DIM, :]
            avt_ref[0, h * MOBA_VROWS + HEAD_DIM:(h + 1) * MOBA_VROWS, rows] = ones
        for ref, z in ((cq_ref, _rope(proj(OFF_CQ, C_W), cos, sin) * (q_scale * LOG2E)),
                       (ck_ref, _rope(proj(OFF_CK, C_W), cos, sin)),
                       (cv_ref, proj(OFF_CV, C_W))):
            for half in range(C_W // V7X_LANES):
                ref[0, half, rows, :] = z[:, half * V7X_LANES:(half + 1) * V7X_LANES]
        for j, off in enumerate((OFF_AG, OFF_BG, OFF_CG, OFF_DG)):
            g = proj(off, A_W)
            zg_ref[0, rows, j * A_W:(j + 1) * A_W] = (g * _sigmoid(g)).astype(BF16)
        zb_ref[0, rows, :] = proj(OFF_BX, B_W)
        dqk = proj(OFF_DQ, 2 * D_KW)
        zd_ref[0, rows, 0:D_KW] = dqk[:, 0:D_KW] * (GLA_DK ** -0.5)
        zd_ref[0, rows, D_KW:2 * D_KW] = dqk[:, D_KW:]
        zd_ref[0, rows, 2 * D_KW:2 * D_KW + D_W] = proj(OFF_DV, D_W)
        dr = proj(OFF_DR, GLA_LOWRANK)
        la = jnp.dot(dr.astype(BF16), wr_ref[...].astype(BF16), preferred_element_type=F32) + br_ref[...]
        zd_ref[0, rows, 2 * D_KW + D_W:] = _log_sigmoid(la) * (1.0 / GLA_TAU)


def _in_projection(x, mod_l, w_in_bf, layer, cos_t, sin_t, gla_wr, gla_br):
    b, s, d = x.shape
    tm = IN_TM
    row3 = lambda w: pl.BlockSpec((1, tm, w), lambda i, j: (i, j, 0))
    const2 = lambda shp: pl.BlockSpec(shp, lambda i, j: (0, 0))
    per_layer = lambda shp: pl.BlockSpec((None,) + shp[1:], lambda i, j: (layer, 0, 0))
    halves = pl.BlockSpec((1, C_W // V7X_LANES, tm, V7X_LANES), lambda i, j: (i, 0, j, 0))
    out_shape = (
        jax.ShapeDtypeStruct((b, s, A_W), BF16),
        jax.ShapeDtypeStruct((b, s, A_W), BF16),
        jax.ShapeDtypeStruct((b, MOBA_HEADS * MOBA_VROWS, s), BF16),
        jax.ShapeDtypeStruct((b, C_W // V7X_LANES, s, V7X_LANES), F32),
        jax.ShapeDtypeStruct((b, C_W // V7X_LANES, s, V7X_LANES), F32),
        jax.ShapeDtypeStruct((b, C_W // V7X_LANES, s, V7X_LANES), F32),
        jax.ShapeDtypeStruct((b, s, 4 * A_W), BF16),
        jax.ShapeDtypeStruct((b, s, B_W), F32),
        jax.ShapeDtypeStruct((b, s, 2 * D_KW + D_W + D_KW), F32),
    )
    out_specs = (row3(A_W), row3(A_W), pl.BlockSpec((1, MOBA_HEADS * MOBA_VROWS, tm), lambda i, j: (i, 0, j)),
                 halves, halves, halves, row3(4 * A_W), row3(B_W), row3(2 * D_KW + D_W + D_KW))
    return pl.pallas_call(
        _inproj_kernel,
        out_shape=out_shape,
        grid=(b, s // tm),
        in_specs=[row3(d),
                  pl.BlockSpec((1, 1, mod_l.shape[-1]), lambda i, j: (i, 0, 0)),
                  per_layer(w_in_bf.shape),
                  row3(V7X_LANES), row3(V7X_LANES),
                  const2(gla_wr.shape), const2((1, D_KW))],
        out_specs=out_specs,
        compiler_params=_params("parallel", "parallel"),
        name="in_projection",
    )(x, mod_l.reshape(b, 1, -1), w_in_bf, cos_t, sin_t, gla_wr, gla_br.reshape(1, D_KW))


def _moba_kernel(q_ref, k_ref, vt_ref, o_ref, kmean_ref, qm_ref, unsel_ref, s_ref, p_ref, alpha_ref, acc_ref):
    blk = MOBA_BLOCK
    nb = k_ref.shape[1] // blk
    n = pl.program_id(1)

    @pl.when(n == 0)
    def _():
        for j in range(nb):
            kj = k_ref[0, j * blk:(j + 1) * blk, :].astype(F32)
            kmean_ref[j:j + 1, :] = jnp.mean(kj, axis=0, keepdims=True)

    q = q_ref[0]
    hq = MOBA_HEADS * blk
    km = kmean_ref[...]
    km_hi = km.astype(BF16)
    km_lo = (km - km_hi.astype(F32)).astype(BF16)
    key_idx = lax.broadcasted_iota(jnp.int32, (blk, blk), 0)
    qry_idx = lax.broadcasted_iota(jnp.int32, (blk, blk), 1)
    causal = key_idx <= qry_idx

    vr = MOBA_VROWS
    vth = lambda h, j: vt_ref[0, h * vr:(h + 1) * vr, pl.ds(pl.multiple_of(j * blk, blk), blk)]
    kblk = lambda j: k_ref[0, pl.ds(pl.multiple_of(j * blk, blk), blk), :]

    q_t = q.astype(F32).T
    row_head = lax.broadcasted_iota(jnp.int32, (A_W, 1), 0) // HEAD_DIM
    for h in range(MOBA_HEADS):
        qm_ref[:, h * blk:(h + 1) * blk] = jnp.where(row_head == h, q_t, 0.0).astype(BF16)
    scores = lambda j: jnp.dot(kblk(j), qm_ref[...], preferred_element_type=F32)

    blk_id = lax.broadcasted_iota(jnp.int32, (nb, MOBA_HEADS * blk), 0)
    own = jnp.dot(jnp.concatenate([kblk(n), km_hi, km_lo], axis=0), qm_ref[...], preferred_element_type=F32)
    s_own = own[0:blk, :]
    gate = jnp.where(blk_id < n, own[blk:blk + nb, :] + own[blk + nb:blk + 2 * nb, :], NEG)
    s_ref[0, :, 0:hq] = scores(0)
    rank = jnp.zeros(gate.shape, jnp.int32)
    for i in range(nb):
        gi = gate[i:i + 1, :]
        tie = (blk_id > i).astype(jnp.int32)
        rank = rank + jnp.where(gi > gate, 1, jnp.where(gi == gate, tie, 0))
    unsel_ref[...] = jnp.where((rank < MOBA_TOPK) & (blk_id < n), 0.0, 1.0)

    def accumulate(slot, j_blk):
        for h in range(MOBA_HEADS):
            cols = slice(h * blk, (h + 1) * blk)
            acc_ref[h * vr:(h + 1) * vr, :] = (alpha_ref[slot, :, cols] * acc_ref[h * vr:(h + 1) * vr, :]
                                              + jnp.dot(vth(h, j_blk), p_ref[slot, :, cols],
                                                        preferred_element_type=F32))

    acc_ref[...] = jnp.zeros_like(acc_ref)
    alpha_ref[1] = jnp.ones(alpha_ref.shape[1:], F32)
    s = jnp.where(jnp.concatenate([causal] * MOBA_HEADS, axis=1), s_own, NEG)
    m_init = jnp.max(s, axis=0, keepdims=True)
    p_ref[1, :, 0:hq] = jnp.exp2(s - m_init).astype(BF16)

    def step(j, j_prev, slot, m, prefetch=True):
        s = s_ref[slot, :, 0:hq]
        accumulate(1 - slot, j_prev)
        if prefetch:
            s_ref[1 - slot, :, 0:hq] = scores(jnp.minimum(j + 1, nb - 1))
        unsel = unsel_ref[pl.ds(j, 1), :] > 0.5
        m_new = jnp.maximum(m, jnp.where(unsel, NEG, jnp.max(s, axis=0, keepdims=True)))
        alpha_ref[slot] = jnp.exp2(m - m_new)
        p_ref[slot, :, 0:hq] = jnp.exp2(s - jnp.where(unsel, -NEG, m_new)).astype(BF16)
        return m_new

    def body(i, m):
        m = step(2 * i, jnp.where(i == 0, n, 2 * i - 1), 0, m)
        return step(2 * i + 1, 2 * i, 1, m)

    m_pairs = lax.fori_loop(0, n // 2, body, m_init)

    @pl.when(n % 2 == 1)
    def _():
        step(n - 1, jnp.where(n == 1, n, n - 2), 0, m_pairs, prefetch=False)
        accumulate(0, n - 1)

    @pl.when(n % 2 == 0)
    def _():
        accumulate(1, jnp.where(n == 0, n, n - 1))
    outs = []
    for h in range(MOBA_HEADS):
        a = acc_ref[h * vr:(h + 1) * vr, :]
        outs.append(a[0:HEAD_DIM, :] / a[HEAD_DIM:HEAD_DIM + 1, :])
    o_ref[0] = jnp.concatenate(outs, axis=0).T.astype(o_ref.dtype)


def _moba(aq, ak, avt):
    b, s, _ = aq.shape
    blk = MOBA_BLOCK
    return pl.pallas_call(
        _moba_kernel,
        out_shape=jax.ShapeDtypeStruct((b, s, A_W), BF16),
        grid=(b, s // blk),
        in_specs=[pl.BlockSpec((1, blk, A_W), lambda i, n: (i, n, 0)),
                  pl.BlockSpec((1, s, A_W), lambda i, n: (i, 0, 0)),
                  pl.BlockSpec((1, MOBA_HEADS * MOBA_VROWS, s), lambda i, n: (i, 0, 0))],
        out_specs=pl.BlockSpec((1, blk, A_W), lambda i, n: (i, n, 0)),
        scratch_shapes=[pltpu.VMEM((s // blk, A_W), F32),
                        pltpu.VMEM((A_W, MOBA_HEADS * blk), BF16),
                        pltpu.VMEM((s // blk, MOBA_HEADS * blk), F32),
                        pltpu.VMEM((2, blk, MOBA_HEADS * blk + V7X_LANES), F32),
                        pltpu.VMEM((2, blk, MOBA_HEADS * blk + V7X_LANES), BF16),
                        pltpu.VMEM((2, 1, MOBA_HEADS * blk), F32),
                        pltpu.VMEM((MOBA_HEADS * MOBA_VROWS, blk), F32)],
        compiler_params=_params("arbitrary", "arbitrary"),
        name="moba_attention",
    )(aq, ak, avt)


def _lru_reset(xs_ref, h_ref):
    xs_ref[0:V7X_SUBLANES, :] = jnp.zeros((V7X_SUBLANES, B_W), F32)
    h_ref[...] = jnp.zeros_like(h_ref)


def _lru_tile(x_ref, cw_ref, cb_ref, wg_ref, bg_ref, lam_ref, o_ref, xs_ref, h_ref):
    tl = x_ref.shape[1]
    halo = V7X_SUBLANES
    x = x_ref[0]
    xs_ref[halo:, :] = x
    y = cb_ref[...] + cw_ref[CONV_WIDTH - 1:CONV_WIDTH, :] * x
    for k in range(CONV_WIDTH - 1):
        y = y + cw_ref[k:k + 1, :] * xs_ref[pl.ds(halo - (CONV_WIDTH - 1) + k, tl), :]
    xs_ref[0:halo, :] = x[tl - halo:, :]

    gates = jnp.dot(y.astype(BF16), wg_ref[...], preferred_element_type=F32) + bg_ref[...]
    r = _sigmoid(gates[:, 0:B_W])
    ig = _sigmoid(gates[:, B_W:])
    lam = lam_ref[...]
    softplus_neg_lam = jnp.maximum(-lam, 0.0) + jnp.log1p(jnp.exp(-jnp.abs(lam)))
    log_a = (-LRU_C) * r * softplus_neg_lam
    a = jnp.exp(log_a)
    u = jnp.sqrt(-jnp.tanh(log_a) * (a * a + 1.0)) * (ig * y)

    row = lax.broadcasted_iota(jnp.int32, (tl, 1), 0)
    step = 1
    while step < tl:
        if step % V7X_SUBLANES:
            keep = row >= step
            a_prev = jnp.where(keep, pltpu.roll(a, step, 0), 1.0)
            u_prev = jnp.where(keep, pltpu.roll(u, step, 0), 0.0)
            u = u + a * u_prev
            a = a * a_prev
        else:
            u = jnp.concatenate([u[:step], u[step:] + a[step:] * u[:tl - step]], axis=0)
            a = jnp.concatenate([a[:step], a[step:] * a[:tl - step]], axis=0)
        step *= 2
    h = u + a * h_ref[...]
    h_ref[...] = h[tl - 1:tl, :]
    o_ref[0] = h.astype(o_ref.dtype)


def _dilated_kernel(q_ref, k_ref, v_ref, o_ref, oacc_ref, lacc_ref):
    qb = DIL_QBLOCK
    tile = o_ref.shape[1]
    t0 = pl.program_id(1) * tile
    halves = range(C_W // V7X_LANES)

    def load(ref, rows):
        return jnp.concatenate([ref.at[half][rows, :] for half in halves], axis=-1)

    def store(ref, rows, value):
        for half in halves:
            ref.at[half][rows, :] = value[:, half * V7X_LANES:(half + 1) * V7X_LANES]

    lane_head = lax.broadcasted_iota(jnp.int32, (1, C_W), 1) // HEAD_DIM
    qi = lax.broadcasted_iota(jnp.int32, (qb, 2 * qb), 0)
    kj = lax.broadcasted_iota(jnp.int32, (qb, 2 * qb), 1)

    for ci, (window, d) in enumerate(DIL_CONFIGS):
        n_steps = window // d
        per_class = tile // (qb * d)
        first, last = ci == 0, ci == len(DIL_CONFIGS) - 1

        def units(i, carry, d=d, n_steps=n_steps, per_class=per_class, first=first, last=last):
            group = []
            for k in range(DIL_UNROLL):
                u = i * DIL_UNROLL + k
                r = u // per_class
                base = t0 // d + (u % per_class) * qb
                start = jnp.maximum(base - qb, 0)
                if d == 1:
                    tokens = lambda m0, count, r=r: pl.ds(pl.multiple_of(m0, qb), count)
                else:
                    tokens = lambda m0, count, r=r: pl.ds(m0 * d + r, count, stride=d)
                q = load(q_ref, tokens(base, qb))
                kk = load(k_ref, tokens(start, 2 * qb)).astype(BF16)
                vv = load(v_ref, tokens(start, 2 * qb)).astype(BF16)
                dist = (base - start) + qi - kj
                valid = (dist >= 0) & (dist <= n_steps)
                q_bf = q.astype(BF16)
                q_stack = jnp.concatenate([jnp.where(lane_head == h, q_bf, jnp.zeros_like(q_bf))
                                           for h in range(DIL_HEADS)], axis=0)
                s = _dot_nt(q_stack, kk)
                group.append((s, valid, vv, tokens(base - t0 // d, qb)))
            soft = []
            for s, valid, vv, local in group:
                s = jnp.where(jnp.concatenate([valid] * DIL_HEADS, axis=0), s, NEG)
                m = jnp.max(s, axis=-1, keepdims=True)
                e = jnp.exp2(s - m)
                soft.append((e.astype(BF16), m, jnp.sum(e, axis=-1, keepdims=True)))
            pvs = [jnp.dot(e, vv, preferred_element_type=F32)
                   for (e, _, _), (_, _, vv, _) in zip(soft, group)]
            for pv, (_, m, l), (_, _, _, local) in zip(pvs, soft, group):
                o_all = pv / l
                lse_all = m + jnp.log2(l)
                out = o_all[0:qb, :]
                lse = jnp.broadcast_to(lse_all[0:qb, :], (qb, C_W))
                for h in range(1, DIL_HEADS):
                    out = jnp.where(lane_head == h, o_all[h * qb:(h + 1) * qb, :], out)
                    lse = jnp.where(lane_head == h, lse_all[h * qb:(h + 1) * qb, :], lse)
                if not first:
                    o_prev = load(oacc_ref, local)
                    l_prev = load(lacc_ref, local)
                    top = jnp.maximum(l_prev, lse)
                    w_prev = jnp.exp2(l_prev - top)
                    w_cur = jnp.exp2(lse - top)
                    out = (w_prev * o_prev + w_cur * out) / (w_prev + w_cur)
                    lse = top + jnp.log2(w_prev + w_cur)
                if last:
                    store(o_ref, local, out)
                else:
                    store(oacc_ref, local, out)
                    store(lacc_ref, local, lse)
            return carry

        lax.fori_loop(0, tile // (qb * DIL_UNROLL), units, 0)


def _dilated(cq, ck, cv):
    b, nh, s, w = cq.shape
    tile = DIL_TILE
    whole = pl.BlockSpec((None, nh, s, w), lambda i, j: (i, 0, 0, 0))
    return pl.pallas_call(
        _dilated_kernel,
        out_shape=jax.ShapeDtypeStruct((b, nh, s, w), F32),
        grid=(b, s // tile),
        in_specs=[whole, whole, whole],
        out_specs=pl.BlockSpec((None, nh, tile, w), lambda i, j: (i, 0, j, 0)),
        scratch_shapes=[pltpu.VMEM((nh, tile, w), F32), pltpu.VMEM((nh, tile, w), F32)],
        compiler_params=_params("parallel", "arbitrary"),
        name="dilated_attention",
    )(cq, ck, cv)


def _gla_tile(z_ref, gn_ref, o_ref, state_ref, b_ref, q_ref, k_ref, v_ref, t_ref, y_ref, side_work):
    c = GLA_CHUNK
    tg = z_ref.shape[1]
    nch = tg // c
    i = pl.program_id(1)

    @pl.when(i == 0)
    def _():
        state_ref[...] = jnp.zeros_like(state_ref)

    qs = z_ref[0, :, 0:D_KW]
    ks = z_ref[0, :, D_KW:2 * D_KW]
    vs = z_ref[0, :, 2 * D_KW:2 * D_KW + D_W]
    la = z_ref[0, :, 2 * D_KW + D_W:]

    v_bf = vs.astype(BF16)
    ri = lax.broadcasted_iota(jnp.int32, (tg, tg), 0)
    ci = lax.broadcasted_iota(jnp.int32, (tg, tg), 1)

    def decays(c):
        chunk_causal = (ri // c == ci // c) & (ci <= ri)
        tri = jnp.where(chunk_causal, 1.0, 0.0).astype(BF16)
        b = sum(jnp.dot(tri, part, preferred_element_type=F32) for part in _split3(la))
        b_last = jnp.broadcast_to(b.reshape(tg // c, c, D_KW)[:, c - 1:c, :], (tg // c, c, D_KW)).reshape(tg, D_KW)
        return chunk_causal, b, b_last

    def through_state(b, b_last, c):
        same_head_t = (lax.broadcasted_iota(jnp.int32, (D_W, D_KW), 0) // GLA_DV
                       == lax.broadcasted_iota(jnp.int32, (D_W, D_KW), 1) // GLA_DK)
        q_in_bf = (qs * jnp.exp(b)).astype(BF16)
        k_dec_bf = (ks * jnp.exp(b_last - b)).astype(BF16)
        st = state_ref[...]
        inter = []
        for ch in range(tg // c):
            r0 = ch * c
            inter.append(_dot_nt(q_in_bf[r0:r0 + c, :], st.astype(BF16)))
            kvt = lax.dot_general(v_bf[r0:r0 + c, :], k_dec_bf[r0:r0 + c, :], (((0,), (0,)), ((), ())),
                                  preferred_element_type=F32)
            st = st * jnp.exp(b_last[r0:r0 + 1, :]) + jnp.where(same_head_t, kvt, 0.0)
        state_ref[...] = st
        y_ref[...] = y_ref[...] + jnp.concatenate(inter, axis=0)

    safe = jnp.min(la) * GLA_FAST_CHUNK > -GLA_SAFE_DECAY

    @pl.when(safe)
    def _():
        side_work()
        chunk_causal, b, b_last = decays(GLA_FAST_CHUNK)
        q_in = qs * jnp.exp(b)
        k_grow = (ks * jnp.exp(-b)).astype(BF16)
        dk_head = lax.broadcasted_iota(jnp.int32, (1, D_KW), 1) // GLA_DK
        q_stack = jnp.concatenate([jnp.where(dk_head == h, q_in, 0.0) for h in range(GLA_HEADS)], axis=0)
        a = _dot_nt(q_stack.astype(BF16), k_grow)
        a = jnp.where(jnp.concatenate([chunk_causal] * GLA_HEADS, axis=0), a, 0.0)
        av = jnp.dot(a.astype(BF16), v_bf, preferred_element_type=F32)
        dv_head = lax.broadcasted_iota(jnp.int32, (1, D_W), 1) // GLA_DV
        y = av[0:tg, :]
        for h in range(1, GLA_HEADS):
            y = jnp.where(dv_head == h, av[h * tg:(h + 1) * tg, :], y)
        y_ref[...] = y
        through_state(b, b_last, GLA_FAST_CHUNK)

    @pl.when(jnp.logical_not(safe))
    def _():
        side_work()
        _, b, b_last = decays(c)
        b_ref[...] = b
        q_ref[...] = qs
        k_ref[...] = ks
        v_ref[...] = vs
        head_sum = (lax.broadcasted_iota(jnp.int32, (D_KW, D_W), 0) // GLA_DK
                    == lax.broadcasted_iota(jnp.int32, (D_KW, D_W), 1) // GLA_DV).astype(BF16)
        jrow = lax.broadcasted_iota(jnp.int32, (c, D_KW), 0)

        def chunk(ch, carry):
            r0 = pl.multiple_of(ch * c, c)
            bch = b_ref[pl.ds(r0, c), :]
            kch = k_ref[pl.ds(r0, c), :]

            def row_terms(ii, carry2):
                bi = b_ref[pl.ds(r0 + ii, 1), :]
                qi = q_ref[pl.ds(r0 + ii, 1), :]
                t = jnp.exp(jnp.minimum(bi - bch, 0.0)) * kch * qi
                t_ref[pl.ds(pl.multiple_of(ii * c, c), c), :] = jnp.where(jrow <= ii, t, 0.0)
                return carry2

            lax.fori_loop(0, c, row_terms, 0)
            a_b = jnp.dot(t_ref[...].astype(BF16), head_sum, preferred_element_type=F32)
            y_ref[pl.ds(r0, c), :] = jnp.sum(a_b.reshape(c, c, D_W) * v_ref[pl.ds(r0, c), :][None, :, :], axis=1)
            return carry

        lax.fori_loop(0, nch, chunk, 0)
        through_state(b, b_last, c)

    y = y_ref[...]

    e_head = jnp.where(lax.broadcasted_iota(jnp.int32, (D_W, D_W), 0) // GLA_DV
                       == lax.broadcasted_iota(jnp.int32, (D_W, D_W), 1) // GLA_DV, 1.0 / GLA_DV, 0.0).astype(BF16)
    yy = y * y
    yy_hi = yy.astype(BF16)
    yy_lo = (yy - yy_hi.astype(F32)).astype(BF16)
    ms = (jnp.dot(yy_hi, e_head, preferred_element_type=F32) + jnp.dot(yy_lo, e_head, preferred_element_type=F32))
    o_ref[0] = (y * lax.rsqrt(ms + LN_EPS) * gn_ref[...]).astype(o_ref.dtype)


def _recurrent_kernel(zb_ref, cw_ref, cb_ref, wg_ref, bg_ref, lam_ref, zd_ref, gn_ref, yb_ref, yd_ref,
                      xs_ref, h_ref, state_ref, b_ref, q_ref, k_ref, v_ref, t_ref, y_ref):
    @pl.when(pl.program_id(1) == 0)
    def _():
        _lru_reset(xs_ref, h_ref)

    lru = functools.partial(_lru_tile, zb_ref, cw_ref, cb_ref, wg_ref, bg_ref, lam_ref, yb_ref, xs_ref, h_ref)
    _gla_tile(zd_ref, gn_ref, yd_ref, state_ref, b_ref, q_ref, k_ref, v_ref, t_ref, y_ref, lru)


def _recurrent_mixers(zb, conv_w, conv_b, w_gates, b_gates, lam, zd, gla_gn):
    b, s, w = zb.shape
    wd = zd.shape[2]
    tg = GLA_TG
    gn_row = jnp.tile(gla_gn.astype(F32), GLA_HEADS).reshape(1, D_W)
    const = lambda shp: pl.BlockSpec(shp, lambda i, j: (0, 0))
    rows = lambda width: pl.BlockSpec((1, tg, width), lambda i, j: (i, j, 0))
    return pl.pallas_call(
        _recurrent_kernel,
        out_shape=(jax.ShapeDtypeStruct((b, s, w), BF16), jax.ShapeDtypeStruct((b, s, D_W), BF16)),
        grid=(b, s // tg),
        in_specs=[rows(w), const(conv_w.shape), const((1, w)), const(w_gates.shape), const((1, 2 * w)),
                  const((1, w)), rows(wd), const((1, D_W))],
        out_specs=(rows(w), rows(D_W)),
        scratch_shapes=[pltpu.VMEM((tg + V7X_SUBLANES, w), F32),
                        pltpu.VMEM((1, w), F32),
                        pltpu.VMEM((D_W, D_KW), F32),
                        pltpu.VMEM((tg, D_KW), F32),
                        pltpu.VMEM((tg, D_KW), F32),
                        pltpu.VMEM((tg, D_KW), F32),
                        pltpu.VMEM((tg, D_W), F32),
                        pltpu.VMEM((GLA_CHUNK * GLA_CHUNK, D_KW), F32),
                        pltpu.VMEM((tg, D_W), F32)],
        compiler_params=_params("arbitrary", "arbitrary"),
        name="rg_lru_and_gla",
    )(zb, conv_w, conv_b.reshape(1, w), w_gates, b_gates.reshape(1, 2 * w), lam.reshape(1, w), zd, gn_row)


def _outproj_kernel(ya_ref, yb_ref, yc_ref, yd_ref, zg_ref, x_ref, mod_ref, w_ref, g_ref, b_ref, o_ref, *, alpha):
    d = x_ref.shape[2]
    y = jnp.zeros((x_ref.shape[1], d), F32)
    yc = jnp.concatenate([yc_ref[0, half] for half in range(yc_ref.shape[1])], axis=-1)
    for j, yj in enumerate((ya_ref[0], yb_ref[0], yc, yd_ref[0])):
        mix = (yj.astype(F32) * zg_ref[0, :, j * A_W:(j + 1) * A_W].astype(F32)).astype(BF16)
        y = y + jnp.dot(mix, w_ref[j * A_W:(j + 1) * A_W, :], preferred_element_type=F32)
    gate = mod_ref[0, :, 2 * d:3 * d]
    r = alpha * x_ref[0] + (1.0 + gate) * y
    mu = jnp.mean(r, axis=-1, keepdims=True)
    rc = r - mu
    var = jnp.mean(rc * rc, axis=-1, keepdims=True)
    o_ref[0] = rc * lax.rsqrt(var + LN_EPS) * g_ref[...] + b_ref[...]


def _out_projection(ya, yb, yc, yd, zg, x, mod_l, w_out_bf, layer, ln_g, ln_b, alpha):
    b, s, d = x.shape
    tm = OUT_TM
    row3 = lambda w: pl.BlockSpec((1, tm, w), lambda i, j: (i, j, 0))
    const2 = lambda shp: pl.BlockSpec(shp, lambda i, j: (0, 0))
    return pl.pallas_call(
        functools.partial(_outproj_kernel, alpha=alpha),
        out_shape=jax.ShapeDtypeStruct((b, s, d), F32),
        grid=(b, s // tm),
        in_specs=[row3(A_W), row3(B_W),
                  pl.BlockSpec((1, yc.shape[1], tm, yc.shape[3]), lambda i, j: (i, 0, j, 0)),
                  row3(D_W), row3(4 * A_W), row3(d),
                  pl.BlockSpec((1, 1, mod_l.shape[-1]), lambda i, j: (i, 0, 0)),
                  pl.BlockSpec((None,) + w_out_bf.shape[1:], lambda i, j: (layer, 0, 0)),
                  const2((1, d)), const2((1, d))],
        out_specs=row3(d),
        compiler_params=_params("parallel", "parallel"),
        name="out_projection",
    )(ya, yb, yc, yd, zg, x, mod_l.reshape(b, 1, -1), w_out_bf, ln_g.reshape(1, d), ln_b.reshape(1, d))


def _block_diag(w):
    g, bd, _ = w.shape
    eye = jnp.eye(g, dtype=w.dtype)
    return (eye[:, None, :, None] * w[:, :, None, :]).reshape(g * bd, g * bd)


def kernel(x, c, positions, w_mod, b_mod, w_in, conv_w, conv_b, lru_wa, lru_ba, lru_wx, lru_bx, lru_lam,
           gla_wr, gla_br, gla_gn, w_out, ln_g, ln_b):
    depth = w_mod.shape[0]
    alpha = (2 * depth) ** 0.25
    mod = _modulation(c, w_mod, b_mod)
    cos_t, sin_t = _rope_tables(positions)
    w_in_bf = w_in.astype(BF16)
    w_out_bf = w_out.astype(BF16)
    for l in range(depth):
        aq, ak, avt, cq, ck, cv, zg, zb, zd = _in_projection(x, mod[l], w_in_bf, l, cos_t, sin_t,
                                                               gla_wr[l], gla_br[l])
        ya = _moba(aq, ak, avt)
        w_gates = jnp.concatenate([_block_diag(lru_wa[l]), _block_diag(lru_wx[l])], axis=1).astype(BF16)
        b_gates = jnp.concatenate([lru_ba[l], lru_bx[l]])
        yb, yd = _recurrent_mixers(zb, conv_w[l], conv_b[l], w_gates, b_gates, lru_lam[l], zd, gla_gn[l])
        yc = _dilated(cq, ck, cv)
        x = _out_projection(ya, yb, yc, yd, zg, x, mod[l], w_out_bf, l, ln_g[l], ln_b[l], alpha)
    return x
```
